```python
import math
import jax, jax.numpy as jnp
from jax import lax
import numpy as np

D_MODEL = 1024
BATCH = 8
SEQ = 4096
DEPTH = 2

HEAD_DIM = 128
QBLOCK = 128
A_HEADS = 4
A_KV_RANK = 256
IDX_HEADS = 4
IDX_DIM = 64
DSA_TOPK = 256
B_HEADS = 4
CMP_LEN = 32
CMP_STRIDE = 16
SEL_BLOCK = 64
N_SEL = 8
WINDOW = 512
C_HEADS = 8
NUM_BUCKETS = 32
MAX_DISTANCE = 128
D_FF = 2816
N_EXPERTS = 8
TOP_K = 2
EXPERT_FF = 3584
RMS_EPS = 1e-6
NEG_BIG = -1e30

N_EVEN = (DEPTH + 1) // 2
N_ODD = DEPTH // 2
EVEN_SPLITS = (A_HEADS * HEAD_DIM, A_KV_RANK, IDX_HEADS * IDX_DIM, IDX_DIM, IDX_HEADS,
               B_HEADS * HEAD_DIM, 6 * HEAD_DIM, 3 * B_HEADS)
EVEN_IN = sum(EVEN_SPLITS)
MIX_EVEN = (A_HEADS + B_HEADS) * HEAD_DIM
MIX_ODD = C_HEADS * HEAD_DIM

kernel_name = 'hybrid_dsa_nsa_stickbreak_moe'


def rmsnorm(x, g):
    xf = x.astype(jnp.float32)
    y = xf * lax.rsqrt(jnp.mean(xf * xf, axis=-1, keepdims=True) + RMS_EPS)
    return (y * g.astype(jnp.float32)).astype(x.dtype)


def rel_bucket(dist):
    n = jnp.maximum(dist, 0)
    max_exact = NUM_BUCKETS // 2
    large = max_exact + (jnp.log(jnp.maximum(n, 1).astype(jnp.float32) / max_exact)
                         / math.log(MAX_DISTANCE / max_exact)
                         * (NUM_BUCKETS - max_exact)).astype(jnp.int32)
    large = jnp.minimum(large, NUM_BUCKETS - 1)
    return jnp.where(n < max_exact, n, large)


def masked_softmax(logits, mask):
    lf = jnp.where(mask, logits.astype(jnp.float32), NEG_BIG)
    return jnp.where(mask, jax.nn.softmax(lf, axis=-1), 0.0)


def to_blocks(a, nb):
    return jnp.moveaxis(a.reshape((a.shape[0], nb, QBLOCK) + a.shape[2:]), 1, 0)


def from_blocks(a):
    nb, b, qb = a.shape[:3]
    return jnp.moveaxis(a, 0, 1).reshape((b, nb * qb) + a.shape[3:])


def gather_rows(table, idx):
    return jax.vmap(lambda tb, ii: tb[ii])(table, idx)


def swiglu(h, wg, wu, wd):
    return (jax.nn.silu(h @ wg) * (h @ wu)) @ wd


def even_mixer(h, w_in, kv_norm, w_uk, w_uv, cmp_pos, cmp_w1, cmp_w2, w_out, rel_bias):
    b, s, _ = h.shape
    nb = s // QBLOCK
    offs = [int(o) for o in np.cumsum(EVEN_SPLITS)[:-1]]
    a_q, a_lat, i_q, i_k, i_w, b_q, b_kv, b_g = jnp.split(h @ w_in, offs, axis=-1)
    a_q = a_q.reshape(b, s, A_HEADS, HEAD_DIM)
    c_kv = rmsnorm(a_lat, kv_norm)
    a_k = c_kv @ w_uk
    a_v = c_kv @ w_uv
    i_q = i_q.reshape(b, s, IDX_HEADS, IDX_DIM)
    i_w = i_w * (IDX_HEADS ** -0.5)
    b_q = b_q.reshape(b, s, B_HEADS, HEAD_DIM)
    b_kv = b_kv.reshape(b, s, 6, HEAD_DIM)
    ck_raw, cv_raw, sk, sv, wk, wv = [b_kv[:, :, j] for j in range(6)]
    b_g = jax.nn.sigmoid(b_g.reshape(b, s, 3, B_HEADS))

    n_cmp = (s - CMP_LEN) // CMP_STRIDE + 1
    cmp_start = jnp.arange(n_cmp, dtype=jnp.int32) * CMP_STRIDE
    cmp_tok = cmp_start[:, None] + jnp.arange(CMP_LEN, dtype=jnp.int32)[None, :]
    cmp_end = cmp_start + CMP_LEN - 1

    def compress(kv, j):
        blk = kv[:, cmp_tok] + cmp_pos[j]
        hid = jax.nn.gelu(blk.reshape(b, n_cmp, CMP_LEN * HEAD_DIM) @ cmp_w1[j])
        return hid @ cmp_w2[j]

    ck = compress(ck_raw, 0)
    cv = compress(cv_raw, 1)

    n_sb = s // SEL_BLOCK
    n_sel = min(N_SEL, n_sb)
    sb_start = jnp.arange(n_sb, dtype=jnp.int32) * SEL_BLOCK
    overlap = ((cmp_start[:, None] < sb_start[None, :] + SEL_BLOCK)
               & (cmp_start[:, None] + CMP_LEN > sb_start[None, :])).astype(jnp.float32)
    sk_blocks = sk.reshape(b, n_sb, SEL_BLOCK, HEAD_DIM)
    sv_blocks = sv.reshape(b, n_sb, SEL_BLOCK, HEAD_DIM)
    blk_ids = jnp.arange(n_sb, dtype=jnp.int32)
    sel_off = jnp.arange(SEL_BLOCK, dtype=jnp.int32)

    pad = jnp.zeros((b, WINDOW, HEAD_DIM), wk.dtype)
    wk_pad = jnp.concatenate([pad, wk], axis=1)
    wv_pad = jnp.concatenate([pad, wv], axis=1)
    win_off = jnp.arange(WINDOW + QBLOCK, dtype=jnp.int32)

    k_dsa = min(DSA_TOPK, s // 4)
    key_pos = jnp.arange(s, dtype=jnp.int32)
    bias_a = rel_bias[:, :A_HEADS]
    bias_b = rel_bias[:, A_HEADS:]
    scale = HEAD_DIM ** -0.5
    idx_scale = IDX_DIM ** -0.5

    def block_fn(args):
        qi, qa, iq, iw, qb, g = args
        t = qi * QBLOCK + jnp.arange(QBLOCK, dtype=jnp.int32)
        isc = jax.nn.relu(jnp.einsum('bqhd,bsd->bqhs', iq, i_k) * idx_scale)
        isc = jnp.einsum('bqhs,bqh->bqs', isc, iw).astype(jnp.float32)
        isc = jnp.where(key_pos[None, None, :] <= t[None, :, None], isc, -jnp.inf)
        _, sel_a = lax.top_k(isc, k_dsa)
        ka = gather_rows(a_k, sel_a)
        va = gather_rows(a_v, sel_a)
        la = (jnp.einsum('bqhd,bqkd->bhqk', qa, ka) * scale
              + jnp.moveaxis(bias_a[rel_bucket(t[None, :, None] - sel_a)], -1, 1))
        pa = masked_softmax(la, (sel_a <= t[None, :, None])[:, None])
        oa = jnp.einsum('bhqk,bqkd->bqhd', pa, va)
        lc = (jnp.einsum('bqhd,bnd->bhqn', qb, ck) * scale
              + jnp.moveaxis(bias_b[rel_bucket(t[:, None] - cmp_end[None, :])], -1, 0))
        pc = masked_softmax(lc, (cmp_end[None, :] <= t[:, None])[None, None])
        oc = jnp.einsum('bhqn,bnd->bqhd', pc, cv)
        imp = jnp.einsum('bhqn,nm->bqm', pc, overlap)
        cur = t // SEL_BLOCK
        forced = ((blk_ids[None, :] == 0) | (blk_ids[None, :] == cur[:, None])
                  | (blk_ids[None, :] == cur[:, None] - 1))
        imp = jnp.where(forced[None], jnp.inf,
                        jnp.where(blk_ids[None, None, :] > cur[None, :, None], -jnp.inf, imp))
        _, sel_b = lax.top_k(imp, n_sel)
        ks = gather_rows(sk_blocks, sel_b).reshape(b, QBLOCK, n_sel * SEL_BLOCK, HEAD_DIM)
        vs = gather_rows(sv_blocks, sel_b).reshape(b, QBLOCK, n_sel * SEL_BLOCK, HEAD_DIM)
        tok = (sel_b[..., None] * SEL_BLOCK + sel_off).reshape(b, QBLOCK, n_sel * SEL_BLOCK)
        ls = (jnp.einsum('bqhd,bqkd->bhqk', qb, ks) * scale
              + jnp.moveaxis(bias_b[rel_bucket(t[None, :, None] - tok)], -1, 1))
        ps = masked_softmax(ls, (tok <= t[None, :, None])[:, None])
        osel = jnp.einsum('bhqk,bqkd->bqhd', ps, vs)
        kw = lax.dynamic_slice_in_dim(wk_pad, qi * QBLOCK, WINDOW + QBLOCK, axis=1)
        vw = lax.dynamic_slice_in_dim(wv_pad, qi * QBLOCK, WINDOW + QBLOCK, axis=1)
        wpos = qi * QBLOCK - WINDOW + win_off
        dist = t[:, None] - wpos[None, :]
        lw = (jnp.einsum('bqhd,bkd->bhqk', qb, kw) * scale
              + jnp.moveaxis(bias_b[rel_bucket(dist)], -1, 0))
        pw = masked_softmax(lw, ((wpos[None, :] >= 0) & (dist >= 0) & (dist < WINDOW))[None, None])
        ow = jnp.einsum('bhqk,bkd->bqhd', pw, vw)
        ob = (g[:, :, 0, :, None] * oc + g[:, :, 1, :, None] * osel
              + g[:, :, 2, :, None] * ow)
        o = jnp.concatenate([oa, ob], axis=2).reshape(b, QBLOCK, MIX_EVEN)
        return o.astype(h.dtype)

    xs = (jnp.arange(nb, dtype=jnp.int32), to_blocks(a_q, nb), to_blocks(i_q, nb),
          to_blocks(i_w, nb), to_blocks(b_q, nb), to_blocks(b_g, nb))
    o = from_blocks(lax.map(block_fn, xs))
    return o @ w_out


def odd_mixer(h, w_qkv, w_out):
    b, s, _ = h.shape
    nb = s // QBLOCK
    q, k, v = jnp.split(h @ w_qkv, 3, axis=-1)
    q = q.reshape(b, s, C_HEADS, HEAD_DIM)
    k = k.reshape(b, s, C_HEADS, HEAD_DIM)
    v = v.reshape(b, s, C_HEADS, HEAD_DIM)
    key_pos = jnp.arange(s, dtype=jnp.int32)
    scale = HEAD_DIM ** -0.5

    def block_fn(args):
        qi, qb = args
        t = qi * QBLOCK + jnp.arange(QBLOCK, dtype=jnp.int32)
        z = jnp.einsum('bqhd,bshd->bhqs', qb, k).astype(jnp.float32) * scale
        strict = key_pos[None, :] < t[:, None]
        log_1m = jnp.where(strict, jax.nn.log_sigmoid(-z), 0.0)
        after = lax.cumsum(log_1m, axis=3, reverse=True) - log_1m
        a = jnp.where(strict, jnp.exp(jax.nn.log_sigmoid(z) + after), 0.0)
        o = jnp.einsum('bhqs,bshd->bqhd', a, v)
        return o.reshape(b, QBLOCK, MIX_ODD).astype(h.dtype)

    o = from_blocks(lax.map(block_fn, (jnp.arange(nb, dtype=jnp.int32), to_blocks(q, nb))))
    return o @ w_out


def moe_swiglu(h, w_router, w_gate, w_up, w_down):
    logits = (h @ w_router).astype(jnp.float32)
    top_v, top_i = lax.top_k(logits, TOP_K)
    top_w = jax.nn.softmax(top_v, axis=-1)
    gates = jnp.sum(jax.nn.one_hot(top_i, N_EXPERTS, dtype=jnp.float32) * top_w[..., None], axis=-2)
    gates = gates.astype(h.dtype)
    y = jnp.zeros_like(h)
    for e in range(N_EXPERTS):
        y = y + gates[..., e:e + 1] * swiglu(h, w_gate[e], w_up[e], w_down[e])
    return y


def setup_inputs(seed: int = 0) -> dict:
    key = jax.random.key(seed)
    ks = jax.random.split(key, 32)
    f32 = jnp.float32

    def nrm(k, shape, scale):
        return jax.random.normal(k, shape, f32) * scale

    def gain(k, shape):
        return 1.0 + 0.02 * jax.random.normal(k, shape, f32)

    hd = HEAD_DIM
    return {
        'x': nrm(ks[0], (BATCH, SEQ, D_MODEL), 1.0),
        'rel_bias': nrm(ks[1], (NUM_BUCKETS, A_HEADS + B_HEADS), 0.2),
        'ev_norm_mix': gain(ks[2], (N_EVEN, D_MODEL)),
        'ev_w_in': nrm(ks[3], (N_EVEN, D_MODEL, EVEN_IN), D_MODEL ** -0.5),
        'ev_kv_norm': gain(ks[4], (N_EVEN, A_KV_RANK)),
        'ev_w_uk': nrm(ks[5], (N_EVEN, A_KV_RANK, hd), A_KV_RANK ** -0.5),
        'ev_w_uv': nrm(ks[6], (N_EVEN, A_KV_RANK, hd), A_KV_RANK ** -0.5),
        'ev_cmp_pos': nrm(ks[7], (N_EVEN, 2, CMP_LEN, hd), 0.02),
        'ev_cmp_w1': nrm(ks[8], (N_EVEN, 2, CMP_LEN * hd, hd), (CMP_LEN * hd) ** -0.5),
        'ev_cmp_w2': nrm(ks[9], (N_EVEN, 2, hd, hd), hd ** -0.5),
        'ev_w_out': nrm(ks[10], (N_EVEN, MIX_EVEN, D_MODEL), MIX_EVEN ** -0.5),
        'ev_norm_ffn': gain(ks[11], (N_EVEN, D_MODEL)),
        'ev_w_gate': nrm(ks[12], (N_EVEN, D_MODEL, D_FF), D_MODEL ** -0.5),
        'ev_w_up': nrm(ks[13], (N_EVEN, D_MODEL, D_FF), D_MODEL ** -0.5),
        'ev_w_down': nrm(ks[14], (N_EVEN, D_FF, D_MODEL), D_FF ** -0.5),
        'od_norm_mix': gain(ks[15], (N_ODD, D_MODEL)),
        'od_w_qkv': nrm(ks[16], (N_ODD, D_MODEL, 3 * MIX_ODD), D_MODEL ** -0.5),
        'od_w_out': nrm(ks[17], (N_ODD, MIX_ODD, D_MODEL), MIX_ODD ** -0.5),
        'od_norm_ffn': gain(ks[18], (N_ODD, D_MODEL)),
        'od_w_router': nrm(ks[19], (N_ODD, D_MODEL, N_EXPERTS), D_MODEL ** -0.5),
        'od_w_gate': nrm(ks[20], (N_ODD, N_EXPERTS, D_MODEL, EXPERT_FF), D_MODEL ** -0.5),
        'od_w_up': nrm(ks[21], (N_ODD, N_EXPERTS, D_MODEL, EXPERT_FF), D_MODEL ** -0.5),
        'od_w_down': nrm(ks[22], (N_ODD, N_EXPERTS, EXPERT_FF, D_MODEL), EXPERT_FF ** -0.5),
        'final_norm': gain(ks[23], (D_MODEL,)),
    }


def reference(x, rel_bias, ev_norm_mix, ev_w_in, ev_kv_norm, ev_w_uk, ev_w_uv, ev_cmp_pos,
              ev_cmp_w1, ev_cmp_w2, ev_w_out, ev_norm_ffn, ev_w_gate, ev_w_up, ev_w_down,
              od_norm_mix, od_w_qkv, od_w_out, od_norm_ffn, od_w_router, od_w_gate, od_w_up,
              od_w_down, final_norm):
    for layer in range(DEPTH):
        i = layer // 2
        if layer % 2 == 0:
            x = x + even_mixer(rmsnorm(x, ev_norm_mix[i]), ev_w_in[i], ev_kv_norm[i], ev_w_uk[i],
                               ev_w_uv[i], ev_cmp_pos[i], ev_cmp_w1[i], ev_cmp_w2[i], ev_w_out[i],
                               rel_bias)
            x = x + swiglu(rmsnorm(x, ev_norm_ffn[i]), ev_w_gate[i], ev_w_up[i], ev_w_down[i])
        else:
            x = x + odd_mixer(rmsnorm(x, od_norm_mix[i]), od_w_qkv[i], od_w_out[i])
            x = x + moe_swiglu(rmsnorm(x, od_norm_ffn[i]), od_w_router[i], od_w_gate[i],
                               od_w_up[i], od_w_down[i])
    return rmsnorm(x, final_norm)
```

```python
import functools
import math

import numpy as np
import jax
import jax.numpy as jnp
from jax import lax
from jax.experimental import pallas as pl
from jax.experimental.pallas import tpu as pltpu

HEAD_DIM = 128
A_HEADS = 4
A_KV_RANK = 256
IDX_HEADS = 4
IDX_DIM = 64
DSA_TOPK = 256
B_HEADS = 4
CMP_LEN = 32
CMP_STRIDE = 16
SEL_BLOCK = 64
N_SEL = 8
WINDOW = 512
C_HEADS = 8
NUM_BUCKETS = 32
MAX_DISTANCE = 128
N_EXPERTS = 8
RMS_EPS = 1e-6
NEG_BIG = -1e30

LANES = 128
CDT = jnp.bfloat16
F32 = jnp.float32
I32 = jnp.int32
INT_MIN = -2 ** 31
KEY_NEG_INF = -2 ** 31 + 0x7FFFFF
VMEM_LIMIT = 56 * 1024 * 1024

TQ = 256
KC = 256
SB_G = 128


def _cparams(sem):
    return pltpu.CompilerParams(dimension_semantics=sem, vmem_limit_bytes=VMEM_LIMIT)


def _dot(a, b):
    return jnp.dot(a, b, preferred_element_type=F32)


def _dot_nt(a, b):
    return lax.dot_general(a, b, (((1,), (1,)), ((), ())), preferred_element_type=F32)


def _rms(x, g):
    return x * lax.rsqrt(jnp.mean(x * x, axis=-1, keepdims=True) + RMS_EPS) * g


def _split_terms(x, n):
    if CDT == F32:
        return [x]
    out, r = [], x
    for _ in range(n):
        h = r.astype(CDT)
        out.append(h)
        r = r - h.astype(F32)
    return out


def _dot_split(x, w01, n):
    terms = _split_terms(x, n)
    return _dot(jnp.concatenate(terms, axis=1), jnp.concatenate([w01] * len(terms), axis=0))


def _sortable(v):
    bits = lax.bitcast_convert_type(v, I32)
    key = bits ^ ((bits >> 31) & 0x7FFFFFFF)
    return jnp.where(v == 0.0, 0, key)


def _stack_heads(x, n, w):
    return jnp.concatenate([x[:, h * w:(h + 1) * w] for h in range(n)], axis=0)


def _unstack_heads(x, n, t):
    return jnp.concatenate([x[h * t:(h + 1) * t] for h in range(n)], axis=1)


def _bucket_np(dist):
    n = np.maximum(dist, 0)
    max_exact = NUM_BUCKETS // 2
    large = max_exact + (np.log(np.maximum(n, 1).astype(np.float32) / np.float32(max_exact))
                         / np.float32(math.log(MAX_DISTANCE / max_exact))
                         * np.float32(NUM_BUCKETS - max_exact)).astype(np.int32)
    large = np.minimum(large, NUM_BUCKETS - 1)
    return np.where(n < max_exact, n, large).astype(np.int32)


def _bias_tile_kernel(bias_ref, tokbk_ref, cmpbk_ref, tok_ref, cmp_ref, *, n_heads):
    tb = tokbk_ref[...]
    cb = cmpbk_ref[...]
    for h in range(n_heads):
        far = bias_ref[NUM_BUCKETS - 1, h]
        tacc = jnp.zeros(tb.shape, F32)
        cacc = jnp.zeros(cb.shape, F32)
        for k in range(NUM_BUCKETS - 1):
            v = bias_ref[k, h] - far
            tacc = jnp.where(tb == k, v, tacc)
            cacc = jnp.where(cb == k, v, cacc)
        tok_ref[h * TQ:(h + 1) * TQ, :] = tacc
        cmp_ref[h * TQ:(h + 1) * TQ, :] = cacc


def _bias_tiles(rel_bias):
    n_heads = rel_bias.shape[1]
    i = np.arange(TQ)[:, None]
    j = np.arange(2 * KC)[None, :]
    dist = i + KC - j
    tokbk = np.where(dist >= 0, _bucket_np(dist), NUM_BUCKETS - 1).astype(np.int32)
    r = np.arange(LANES)[None, :]
    cdist = i + (TQ - CMP_LEN + 1) - CMP_STRIDE * r
    cmpbk = np.where((cdist >= 0) & (r < 2 * TQ // CMP_STRIDE), _bucket_np(cdist), NUM_BUCKETS - 1).astype(np.int32)
    return pl.pallas_call(
        functools.partial(_bias_tile_kernel, n_heads=n_heads),
        out_shape=(jax.ShapeDtypeStruct((n_heads * TQ, 2 * KC), F32),
                   jax.ShapeDtypeStruct((n_heads * TQ, LANES), F32)),
        in_specs=[pl.BlockSpec(memory_space=pltpu.SMEM),
                  pl.BlockSpec(memory_space=pltpu.VMEM),
                  pl.BlockSpec(memory_space=pltpu.VMEM)],
        out_specs=(pl.BlockSpec(memory_space=pltpu.VMEM), pl.BlockSpec(memory_space=pltpu.VMEM)),
        name="bias_tiles",
    )(rel_bias, jnp.asarray(tokbk), jnp.asarray(cmpbk))


EV_AQ = A_HEADS * HEAD_DIM
EV_IQ = IDX_HEADS * IDX_DIM
EV_BQ = B_HEADS * HEAD_DIM
EV_OFF_AQ = 0
EV_OFF_LAT = EV_OFF_AQ + EV_AQ
EV_OFF_IQ = EV_OFF_LAT + A_KV_RANK
EV_OFF_BQ = EV_OFF_IQ + EV_IQ
EV_OFF_KV = EV_OFF_BQ + EV_BQ
EV_OFF_SMALL = EV_OFF_KV + 6 * HEAD_DIM
EV_COLS = EV_OFF_SMALL + LANES
SM_IK = 0
SM_IW = IDX_DIM
SM_G = IDX_DIM + IDX_HEADS


def _even_inproj_kernel(x_ref, g_ref, w_ref, kvg_ref, wuk_ref, wuv_ref,
                        aq_ref, ak_ref, av_ref, iq_ref, ik_ref, bq_ref,
                        ckr_ref, cvr_ref, sk_ref, sv_ref, wk_ref, wv_ref, small_ref):
    h = _rms(x_ref[...], g_ref[...]).astype(CDT)

    def seg(off, width):
        return _dot(h, w_ref[:, off:off + width])

    aq_ref[...] = seg(EV_OFF_AQ, EV_AQ).astype(aq_ref.dtype)
    lat = seg(EV_OFF_LAT, A_KV_RANK)
    ckv = _rms(lat, kvg_ref[...]).astype(CDT)
    ak_ref[...] = _dot(ckv, wuk_ref[...]).astype(ak_ref.dtype)
    av_ref[...] = _dot(ckv, wuv_ref[...]).astype(av_ref.dtype)
    iq_ref[...] = seg(EV_OFF_IQ, EV_IQ).astype(iq_ref.dtype)
    bq_ref[...] = seg(EV_OFF_BQ, EV_BQ).astype(bq_ref.dtype)
    kv = seg(EV_OFF_KV, 6 * HEAD_DIM)
    for j, ref in enumerate((ckr_ref, cvr_ref, sk_ref, sv_ref, wk_ref, wv_ref)):
        ref[...] = kv[:, j * HEAD_DIM:(j + 1) * HEAD_DIM].astype(ref.dtype)
    small = seg(EV_OFF_SMALL, LANES)
    small_ref[...] = small
    ik_ref[...] = small[:, SM_IK:SM_IK + IDX_DIM].astype(ik_ref.dtype)


def _even_inproj(x2d, g, w_in, kv_norm, w_uk, w_uv, tm=512):
    m, d = x2d.shape
    offs = np.cumsum((EV_AQ, A_KV_RANK, EV_IQ, IDX_DIM, IDX_HEADS, EV_BQ, 6 * HEAD_DIM, 3 * B_HEADS))
    aq, lat, iq, ik, iw, bq, bkv, bg = jnp.split(w_in, [int(o) for o in offs[:-1]], axis=1)
    pad = jnp.zeros((d, LANES - IDX_DIM - IDX_HEADS - 3 * B_HEADS), w_in.dtype)
    w = jnp.concatenate([aq, lat, iq, bq, bkv, ik, iw, bg, pad], axis=1).astype(CDT)
    row = lambda width: pl.BlockSpec((tm, width), lambda i: (i, 0))
    full = lambda a: pl.BlockSpec(a.shape, lambda i: (0,) * a.ndim)
    g2, kvg2 = g.reshape(1, d), kv_norm.reshape(1, A_KV_RANK)
    wuk, wuv = w_uk.astype(CDT), w_uv.astype(CDT)
    widths = (EV_AQ, HEAD_DIM, HEAD_DIM, EV_IQ, IDX_DIM, EV_BQ) + (HEAD_DIM,) * 6 + (LANES,)
    dtypes = (CDT,) * 6 + (F32, F32) + (CDT,) * 4 + (F32,)
    return pl.pallas_call(
        _even_inproj_kernel,
        grid=(m // tm,),
        in_specs=[row(d), full(g2), full(w), full(kvg2), full(wuk), full(wuv)],
        out_specs=tuple(row(wd) for wd in widths),
        out_shape=tuple(jax.ShapeDtypeStruct((m, wd), dt) for wd, dt in zip(widths, dtypes)),
        compiler_params=_cparams(("parallel",)),
        name="even_inproj",
    )(x2d, g2, w, kvg2, wuk, wuv)


def _compress_kernel(kv_ref, pos_ref, w1_ref, w2_ref, o_ref):
    kv = kv_ref[0, 0]
    half = kv.shape[1]
    first = _dot((kv + pos_ref[0, :, :half]).astype(CDT), w1_ref[0, :half, :])
    second = _dot((kv + pos_ref[0, :, half:]).astype(CDT), w1_ref[0, half:, :])
    hid = first + pltpu.roll(second, second.shape[0] - 1, 0)
    hid = jax.nn.gelu(hid)
    o_ref[0, 0] = _dot(hid.astype(CDT), w2_ref[0]).astype(o_ref.dtype)


def _compress(raw, cmp_pos, cmp_w1, cmp_w2):
    _, b, s, hd = raw.shape
    g = s // CMP_STRIDE
    kv = raw.reshape(2, b, g, CMP_STRIDE * hd)
    pos = cmp_pos.reshape(2, 1, CMP_LEN * hd)
    return pl.pallas_call(
        _compress_kernel,
        grid=(2, b),
        in_specs=[pl.BlockSpec((1, 1, g, CMP_STRIDE * hd), lambda j, i: (j, i, 0, 0)),
                  pl.BlockSpec((1, 1, CMP_LEN * hd), lambda j, i: (j, 0, 0)),
                  pl.BlockSpec((1, CMP_LEN * hd, hd), lambda j, i: (j, 0, 0)),
                  pl.BlockSpec((1, hd, hd), lambda j, i: (j, 0, 0))],
        out_specs=pl.BlockSpec((1, 1, g, hd), lambda j, i: (j, i, 0, 0)),
        out_shape=jax.ShapeDtypeStruct((2, b, g, hd), CDT),
        compiler_params=_cparams(("parallel", "parallel")),
        name="nsa_compress",
    )(kv, pos, cmp_w1.astype(CDT), cmp_w2.astype(CDT))


def _softmax_step(carry, logits, mask4, v):
    m_i, l_i, acc = carry
    s = jnp.where(mask4, logits, NEG_BIG)
    m_new = jnp.maximum(m_i, jnp.max(s, axis=1, keepdims=True))
    alpha = jnp.exp(m_i - m_new)
    p = jnp.where(mask4, jnp.exp(s - m_new), 0.0)
    l_new = alpha * l_i + jnp.sum(p, axis=1, keepdims=True)
    acc = alpha * acc + _dot(p.astype(CDT), v)
    return m_new, l_new, acc


def _softmax_init(rows):
    return (jnp.full((rows, 1), NEG_BIG, F32), jnp.zeros((rows, 1), F32), jnp.zeros((rows, HEAD_DIM), F32))


def _softmax_out(carry):
    _, l_i, acc = carry
    return jnp.where(l_i > 0.0, acc / jnp.where(l_i > 0.0, l_i, 1.0), 0.0)


def _tile4(mask):
    return jnp.concatenate([mask] * 4, axis=0)


def _dsa_kernel(aq_ref, iq_ref, small_ref, ik_ref, ak_ref, av_ref, tokb_ref, o_ref, key_scr, j_scr,
                *, seq, topk):
    m = pl.program_id(1)
    scale = HEAD_DIM ** -0.5
    q_stack = _stack_heads(aq_ref[0], A_HEADS, HEAD_DIM)
    iq_stack = _stack_heads(iq_ref[0], IDX_HEADS, IDX_DIM)
    iw = small_ref[0][:, SM_IW:SM_IW + IDX_HEADS] * (IDX_HEADS ** -0.5)
    t_pos = m * TQ + lax.broadcasted_iota(I32, (TQ, 1), 0)
    lane = lax.broadcasted_iota(I32, (1, KC), 1)

    @pl.when(m == 0)
    def _():
        key_scr[...] = jnp.full(key_scr.shape, INT_MIN, I32)

    def chunk(c):
        return pl.ds(pl.multiple_of(c * KC, KC), KC)

    def score_body(c, carry):
        s = _dot_nt(iq_stack, ik_ref[0, chunk(c), :])
        isc = jnp.zeros((TQ, KC), F32)
        for h in range(IDX_HEADS):
            isc = isc + jnp.maximum(s[h * TQ:(h + 1) * TQ] * (IDX_DIM ** -0.5), 0.0) * iw[:, h:h + 1]
        isc = jnp.where(c * KC + lane <= t_pos, isc, -jnp.inf)
        key_scr[:, chunk(c)] = _sortable(isc)
        return carry

    lax.fori_loop(0, m + 1, score_body, 0)

    def count(pred):
        def body(c, acc):
            return acc + jnp.where(pred(key_scr[:, chunk(c)], c * KC + lane), 1.0, 0.0)
        acc = lax.fori_loop(0, m + 1, body, jnp.zeros((TQ, KC), F32))
        return jnp.sum(acc, axis=1, keepdims=True)

    kf = float(topk)
    t0 = jnp.where(count(lambda k, _: k >= 0) >= kf, 0, INT_MIN).astype(I32)

    def bit_body(i, t_cur):
        cand = t_cur | lax.shift_left(jnp.int32(1), 30 - i)
        return jnp.where(count(lambda k, _: k >= cand) >= kf, cand, t_cur)

    thr = lax.fori_loop(0, 31, bit_body, t0)
    n_ge = count(lambda k, _: k >= thr)
    n_gt = count(lambda k, _: k > thr)
    need = kf - n_gt
    tied = (n_ge > kf) & (thr > KEY_NEG_INF)
    j_scr[...] = jnp.full((TQ, 1), seq, I32)

    @pl.when(jnp.max(jnp.where(tied, 1.0, 0.0)) > 0.0)
    def _():
        def jbit(i, j_cur):
            cand = j_cur | lax.shift_left(jnp.int32(1), (seq.bit_length() - 2) - i)
            return jnp.where(count(lambda k, idx: (k == thr) & (idx < cand)) < need, cand, j_cur)
        j_last = lax.fori_loop(0, seq.bit_length() - 1, jbit, jnp.zeros((TQ, 1), I32))
        j_scr[...] = jnp.where(tied, j_last, seq)

    j_last = j_scr[...]

    def attend(c, carry, bias, causal):
        logits = _dot_nt(q_stack, ak_ref[0, chunk(c), :]) * scale
        if bias is not None:
            logits = logits + bias
        k = key_scr[:, chunk(c)]
        idx = c * KC + lane
        sel = (k > thr) | ((k == thr) & (idx <= j_last))
        if causal:
            sel = sel & (idx <= t_pos)
        return _softmax_step(carry, logits, _tile4(sel), av_ref[0, chunk(c), :])

    carry = _softmax_init(A_HEADS * TQ)
    carry = lax.fori_loop(0, jnp.maximum(m - 1, 0), lambda c, cr: attend(c, cr, None, False), carry)
    carry = lax.cond(m >= 1, lambda cr: attend(m - 1, cr, tokb_ref[:, :KC], False), lambda cr: cr, carry)
    carry = attend(m, carry, tokb_ref[:, KC:], True)
    o_ref[0] = _unstack_heads(_softmax_out(carry), A_HEADS, TQ).astype(o_ref.dtype)


def _dsa(aq, iq, small, ik, ak, av, tokb):
    b, s, _ = aq.shape
    topk = min(DSA_TOPK, s // 4)
    qspec = lambda w: pl.BlockSpec((1, TQ, w), lambda i, j: (i, j, 0))
    kspec = lambda w: pl.BlockSpec((1, s, w), lambda i, j: (i, 0, 0))
    return pl.pallas_call(
        functools.partial(_dsa_kernel, seq=s, topk=topk),
        grid=(b, s // TQ),
        in_specs=[qspec(EV_AQ), qspec(EV_IQ), qspec(LANES), kspec(IDX_DIM), kspec(HEAD_DIM), kspec(HEAD_DIM),
                  pl.BlockSpec((A_HEADS * TQ, 2 * KC), lambda i, j: (0, 0))],
        out_specs=qspec(EV_AQ),
        out_shape=jax.ShapeDtypeStruct((b, s, EV_AQ), CDT),
        scratch_shapes=[pltpu.VMEM((TQ, s), I32), pltpu.VMEM((TQ, 1), I32)],
        compiler_params=_cparams(("parallel", "arbitrary")),
        name="dsa_attention",
    )(aq, iq, small, ik, ak, av, tokb)


def _nsa_kernel(bq_ref, small_ref, ck_ref, cv_ref, sk_ref, sv_ref, wk_ref, wv_ref, tokb_ref, cmpb_ref, o_ref,
                *, seq):
    m = pl.program_id(1)
    scale = HEAD_DIM ** -0.5
    n_c = seq // CMP_STRIDE
    n_sb = seq // SEL_BLOCK
    q_stack = _stack_heads(bq_ref[0], B_HEADS, HEAD_DIM)
    t_pos = m * TQ + lax.broadcasted_iota(I32, (TQ, 1), 0)
    lane = lax.broadcasted_iota(I32, (1, KC), 1)

    def chunk(c):
        return pl.ds(pl.multiple_of(c * KC, KC), KC)

    lc = _dot_nt(q_stack, ck_ref[0]) * scale
    r_i = lax.broadcasted_iota(I32, (LANES, n_c), 0)
    n_i = lax.broadcasted_iota(I32, (LANES, n_c), 1)
    first_n = m * (TQ // CMP_STRIDE) - TQ // CMP_STRIDE
    place = jnp.where(n_i == first_n + r_i, 1.0, 0.0).astype(CDT)
    lc = lc + _dot_split(cmpb_ref[...], place, 2)
    n_row = lax.broadcasted_iota(I32, (1, n_c), 1)
    cvalid = (n_row * CMP_STRIDE + (CMP_LEN - 1) <= t_pos) & (n_row < n_c - 1)
    cv4 = _tile4(cvalid)
    sc = jnp.where(cv4, lc, NEG_BIG)
    pc = jnp.where(cv4, jnp.exp(sc - jnp.max(sc, axis=1, keepdims=True)), 0.0)
    den = jnp.sum(pc, axis=1, keepdims=True)
    pc = jnp.where(den > 0.0, pc / jnp.where(den > 0.0, den, 1.0), 0.0)
    oc = _dot(pc.astype(CDT), cv_ref[0])

    psum = pc[0:TQ]
    for h in range(1, B_HEADS):
        psum = psum + pc[h * TQ:(h + 1) * TQ]
    on = lax.broadcasted_iota(I32, (n_c, LANES), 0) * CMP_STRIDE
    om = lax.broadcasted_iota(I32, (n_c, LANES), 1) * SEL_BLOCK
    overlap = jnp.where((on < om + SEL_BLOCK) & (on + CMP_LEN > om) & (om < seq), 1.0, 0.0).astype(CDT)
    imp = _dot_split(psum, overlap, 3)
    blk = lax.broadcasted_iota(I32, (1, LANES), 1)
    cur = t_pos >> (SEL_BLOCK.bit_length() - 1)
    forced = (blk == 0) | (blk == cur) | (blk == cur - 1)
    val = jnp.where(forced, jnp.inf, jnp.where(blk > cur, -jnp.inf, imp))
    blk_f = blk.astype(F32)
    alive = blk < n_sb
    sel = jnp.zeros((TQ, LANES), F32)
    for _ in range(min(N_SEL, n_sb)):
        best = jnp.max(jnp.where(alive, val, -jnp.inf), axis=1, keepdims=True)
        first = jnp.min(jnp.where(alive & (val == best), blk_f, float(LANES)), axis=1, keepdims=True)
        pick = blk_f == first
        sel = jnp.where(pick, 1.0, sel)
        alive = alive & jnp.logical_not(pick)
    sel_c = sel.astype(CDT)

    def near_bias(near):
        return tokb_ref[:, near * KC:(near + 1) * KC]

    def sel_attend(c, carry, near):
        logits = _dot_nt(q_stack, sk_ref[0, chunk(c), :]) * scale
        e_blk = lax.broadcasted_iota(I32, (LANES, KC), 0)
        e_key = lax.broadcasted_iota(I32, (LANES, KC), 1)
        expand = jnp.where(e_blk == c * (KC // SEL_BLOCK) + (e_key >> (SEL_BLOCK.bit_length() - 1)), 1.0, 0.0).astype(CDT)
        mask = _dot(sel_c, expand) > 0.5
        if near is not None:
            logits = logits + near_bias(near)
            mask = mask & (c * KC + lane <= t_pos)
        return _softmax_step(carry, logits, _tile4(mask), sv_ref[0, chunk(c), :])

    cs = _softmax_init(B_HEADS * TQ)
    cs = lax.fori_loop(0, jnp.maximum(m - 1, 0), lambda c, cr: sel_attend(c, cr, None), cs)
    cs = lax.cond(m >= 1, lambda cr: sel_attend(m - 1, cr, 0), lambda cr: cr, cs)
    cs = sel_attend(m, cs, 1)
    osel = _softmax_out(cs)

    def win_attend(c, carry, near):
        logits = _dot_nt(q_stack, wk_ref[0, chunk(c), :]) * scale
        dist = t_pos - (c * KC + lane)
        mask = (dist >= 0) & (dist < WINDOW)
        if near is not None:
            logits = logits + near_bias(near)
        return _softmax_step(carry, logits, _tile4(mask), wv_ref[0, chunk(c), :])

    cw = _softmax_init(B_HEADS * TQ)
    for back in range(WINDOW // KC, 1, -1):
        cw = lax.cond(m >= back, lambda cr, back=back: win_attend(m - back, cr, None), lambda cr: cr, cw)
    cw = lax.cond(m >= 1, lambda cr: win_attend(m - 1, cr, 0), lambda cr: cr, cw)
    cw = win_attend(m, cw, 1)
    ow = _softmax_out(cw)

    gate = jax.nn.sigmoid(small_ref[0][:, SM_G:SM_G + 3 * B_HEADS])
    outs = []
    for h in range(B_HEADS):
        rows = slice(h * TQ, (h + 1) * TQ)
        outs.append(gate[:, h:h + 1] * oc[rows]
                    + gate[:, B_HEADS + h:B_HEADS + h + 1] * osel[rows]
                    + gate[:, 2 * B_HEADS + h:2 * B_HEADS + h + 1] * ow[rows])
    o_ref[0] = jnp.concatenate(outs, axis=1).astype(o_ref.dtype)


def _nsa(bq, small, ck, cv, sk, sv, wk, wv, tokb, cmpb):
    b, s, _ = bq.shape
    n_c = s // CMP_STRIDE
    qspec = lambda w: pl.BlockSpec((1, TQ, w), lambda i, j: (i, j, 0))
    kspec = lambda n: pl.BlockSpec((1, n, HEAD_DIM), lambda i, j: (i, 0, 0))
    return pl.pallas_call(
        functools.partial(_nsa_kernel, seq=s),
        grid=(b, s // TQ),
        in_specs=[qspec(EV_BQ), qspec(LANES), kspec(n_c), kspec(n_c), kspec(s), kspec(s), kspec(s), kspec(s),
                  pl.BlockSpec((B_HEADS * TQ, 2 * KC), lambda i, j: (1, 0)),
                  pl.BlockSpec((B_HEADS * TQ, LANES), lambda i, j: (1, 0))],
        out_specs=qspec(EV_BQ),
        out_shape=jax.ShapeDtypeStruct((b, s, EV_BQ), CDT),
        compiler_params=_cparams(("parallel", "arbitrary")),
        name="nsa_attention",
    )(bq, small, ck, cv, sk, sv, wk, wv, tokb, cmpb)


def _proj_resid_kernel(*refs, n_in):
    a_refs, w_refs, r_ref, o_ref = refs[:n_in], refs[n_in:2 * n_in], refs[2 * n_in], refs[2 * n_in + 1]
    acc = r_ref[...]
    for a_ref, w_ref in zip(a_refs, w_refs):
        acc = acc + _dot(a_ref[...], w_ref[...])
    o_ref[...] = acc


def _proj_resid(acts, weights, resid, tm=512):
    m, d = resid.shape
    row = lambda width: pl.BlockSpec((tm, width), lambda i: (i, 0))
    full = lambda a: pl.BlockSpec(a.shape, lambda i: (0, 0))
    weights = [w.astype(CDT) for w in weights]
    return pl.pallas_call(
        functools.partial(_proj_resid_kernel, n_in=len(acts)),
        grid=(m // tm,),
        in_specs=[row(a.shape[1]) for a in acts] + [full(w) for w in weights] + [row(d)],
        out_specs=row(d),
        out_shape=jax.ShapeDtypeStruct((m, d), F32),
        compiler_params=_cparams(("parallel",)),
        name="proj_residual",
    )(*acts, *weights, resid)


def _rms_matmul_kernel(x_ref, g_ref, w_ref, o_ref):
    h = _rms(x_ref[...], g_ref[...]).astype(CDT)
    o_ref[...] = _dot(h, w_ref[...]).astype(o_ref.dtype)


def _rms_matmul(x2d, g, w, tm=512, tn=1024):
    m, d = x2d.shape
    n = w.shape[1]
    return pl.pallas_call(
        _rms_matmul_kernel,
        grid=(m // tm, n // tn),
        in_specs=[pl.BlockSpec((tm, d), lambda i, j: (i, 0)),
                  pl.BlockSpec((1, d), lambda i, j: (0, 0)),
                  pl.BlockSpec((d, tn), lambda i, j: (0, j))],
        out_specs=pl.BlockSpec((tm, tn), lambda i, j: (i, j)),
        out_shape=jax.ShapeDtypeStruct((m, n), CDT),
        compiler_params=_cparams(("parallel", "parallel")),
        name="rms_matmul",
    )(x2d, g.reshape(1, d), w.astype(CDT))


def _stick_kernel(q_ref, k_ref, v_ref, o_ref):
    m = pl.program_id(2)
    scale = HEAD_DIM ** -0.5
    q = q_ref[0]
    t_pos = m * TQ + lax.broadcasted_iota(I32, (TQ, 1), 0)
    lane = lax.broadcasted_iota(I32, (1, SB_G), 1)
    n_terms = 1 if CDT == F32 else 2
    uj = lax.broadcasted_iota(I32, (SB_G, 2 * SB_G), 0)
    us = lax.broadcasted_iota(I32, (SB_G, 2 * SB_G), 1)
    suffix = jnp.where((us >= SB_G) | (uj >= us), 1.0, 0.0).astype(CDT)

    def granule(j, carry, masked):
        run, acc = carry
        rows = pl.ds(pl.multiple_of(j * SB_G, SB_G), SB_G)
        z = _dot_nt(q, k_ref[0, rows, :]) * scale
        log_1m = -(jnp.maximum(z, 0.0) + jnp.log1p(jnp.exp(-jnp.abs(z))))
        if masked:
            strict = j * SB_G + lane < t_pos
            log_1m = jnp.where(strict, log_1m, 0.0)
        sums = _dot_split(log_1m, suffix, n_terms)
        a = jnp.exp(z + run + sums[:, :SB_G])
        if masked:
            a = jnp.where(strict, a, 0.0)
        acc = acc + _dot(a.astype(CDT), v_ref[0, rows, :])
        return run + sums[:, SB_G:], acc

    per = TQ // SB_G
    carry = (jnp.zeros((TQ, SB_G), F32), jnp.zeros((TQ, HEAD_DIM), F32))
    for g in range(per - 1, -1, -1):
        carry = granule(m * per + g, carry, True)
    n_far = m * per
    carry = lax.fori_loop(0, n_far, lambda i, cr: granule(n_far - 1 - i, cr, False), carry)
    o_ref[0] = carry[1].astype(o_ref.dtype)


def _stick(qkv, b, s):
    h = C_HEADS
    return pl.pallas_call(
        _stick_kernel,
        grid=(b, h, s // TQ),
        in_specs=[pl.BlockSpec((1, TQ, HEAD_DIM), lambda i, j, k: (i, k, j)),
                  pl.BlockSpec((1, s, HEAD_DIM), lambda i, j, k: (i, 0, h + j)),
                  pl.BlockSpec((1, s, HEAD_DIM), lambda i, j, k: (i, 0, 2 * h + j))],
        out_specs=pl.BlockSpec((1, TQ, HEAD_DIM), lambda i, j, k: (i, k, j)),
        out_shape=jax.ShapeDtypeStruct((b, s, h * HEAD_DIM), CDT),
        compiler_params=_cparams(("parallel", "parallel", "arbitrary")),
        name="stick_breaking",
    )(qkv, qkv, qkv)


def _ffn_kernel(*refs, routed, final):
    it = iter(refs)
    x_ref, g_ref = next(it), next(it)
    wr_ref = next(it) if routed else None
    wg_ref, wu_ref, wd_ref = next(it), next(it), next(it)
    fg_ref = next(it) if final else None
    o_ref, h_scr, acc_scr = next(it), next(it), next(it)
    if routed:
        gate_scr, eacc_scr = next(it), next(it)
    e, f = pl.program_id(1), pl.program_id(2)
    n_e, n_f = pl.num_programs(1), pl.num_programs(2)

    @pl.when((e == 0) & (f == 0))
    def _():
        hn = _rms(x_ref[...], g_ref[...])
        h_scr[...] = hn.astype(CDT)
        acc_scr[...] = jnp.zeros(acc_scr.shape, F32)
        if routed:
            logits = jnp.dot(hn, wr_ref[...], preferred_element_type=F32, precision=lax.Precision.HIGHEST)
            lane = lax.broadcasted_iota(I32, logits.shape, 1)
            logits = jnp.where(lane < N_EXPERTS, logits, -jnp.inf)
            v1 = jnp.max(logits, axis=1, keepdims=True)
            i1 = jnp.min(jnp.where(logits == v1, lane, LANES), axis=1, keepdims=True)
            rest = jnp.where(lane == i1, -jnp.inf, logits)
            v2 = jnp.max(rest, axis=1, keepdims=True)
            i2 = jnp.min(jnp.where(rest == v2, lane, LANES), axis=1, keepdims=True)
            e2 = jnp.exp(v2 - v1)
            den = 1.0 + e2
            gate_scr[...] = jnp.where(lane == i1, 1.0 / den, 0.0) + jnp.where(lane == i2, e2 / den, 0.0)

    h = h_scr[...]
    act = jax.nn.silu(_dot(h, wg_ref[0])) * _dot(h, wu_ref[0])
    y = _dot(act.astype(CDT), wd_ref[0])
    if routed:
        @pl.when(f == 0)
        def _():
            eacc_scr[...] = y

        @pl.when(f > 0)
        def _():
            eacc_scr[...] += y

        @pl.when(f == n_f - 1)
        def _():
            lane = lax.broadcasted_iota(I32, gate_scr.shape, 1)
            ge = jnp.sum(jnp.where(lane == e, gate_scr[...], 0.0), axis=1, keepdims=True)
            acc_scr[...] += ge * eacc_scr[...]
    else:
        acc_scr[...] += y

    @pl.when((e == n_e - 1) & (f == n_f - 1))
    def _():
        out = x_ref[...] + acc_scr[...]
        if final:
            out = _rms(out, fg_ref[...])
        o_ref[...] = out


def _ffn(x2d, g, w_gate, w_up, w_down, w_router=None, final_g=None, tm=512, tf=None):
    m, d = x2d.shape
    n_e, _, ff = w_gate.shape
    routed, final = w_router is not None, final_g is not None
    row = pl.BlockSpec((tm, d), lambda i, e, f: (i, 0))
    vec = pl.BlockSpec((1, d), lambda i, e, f: (0, 0))
    ins, specs = [x2d, g.reshape(1, d)], [row, vec]
    if routed:
        wr = jnp.concatenate([w_router, jnp.zeros((d, LANES - n_e), w_router.dtype)], axis=1)
        ins.append(wr)
        specs.append(pl.BlockSpec((d, LANES), lambda i, e, f: (0, 0)))
    ins += [w_gate.astype(CDT), w_up.astype(CDT), w_down.astype(CDT)]
    specs += [pl.BlockSpec((1, d, tf), lambda i, e, f: (e, 0, f)),
              pl.BlockSpec((1, d, tf), lambda i, e, f: (e, 0, f)),
              pl.BlockSpec((1, tf, d), lambda i, e, f: (e, f, 0))]
    if final:
        ins.append(final_g.reshape(1, d))
        specs.append(vec)
    scratch = [pltpu.VMEM((tm, d), CDT), pltpu.VMEM((tm, d), F32)]
    if routed:
        scratch += [pltpu.VMEM((tm, LANES), F32), pltpu.VMEM((tm, d), F32)]
    return pl.pallas_call(
        functools.partial(_ffn_kernel, routed=routed, final=final),
        grid=(m // tm, n_e, ff // tf),
        in_specs=specs,
        out_specs=row,
        out_shape=jax.ShapeDtypeStruct((m, d), F32),
        scratch_shapes=scratch,
        compiler_params=_cparams(("parallel", "arbitrary", "arbitrary")),
        name="routed_swiglu" if routed else "swiglu",
    )(*ins)


def _even_mixer(x2d, b, s, g, w_in, kv_norm, w_uk, w_uv, cmp_pos, cmp_w1, cmp_w2, w_out, tokb, cmpb):
    (aq, ak, av, iq, ik, bq, ckr, cvr, sk, sv, wk, wv, small) = _even_inproj(x2d, g, w_in, kv_norm, w_uk, w_uv)
    r3 = lambda a: a.reshape(b, s, a.shape[1])
    oa = _dsa(r3(aq), r3(iq), r3(small), r3(ik), r3(ak), r3(av), tokb)
    cmp = _compress(jnp.stack([ckr, cvr]).reshape(2, b, s, HEAD_DIM), cmp_pos, cmp_w1, cmp_w2)
    ob = _nsa(r3(bq), r3(small), cmp[0], cmp[1], r3(sk), r3(sv), r3(wk), r3(wv), tokb, cmpb)
    return _proj_resid([oa.reshape(b * s, EV_AQ), ob.reshape(b * s, EV_BQ)],
                       [w_out[:EV_AQ], w_out[EV_AQ:]], x2d)


def kernel(x, rel_bias, ev_norm_mix, ev_w_in, ev_kv_norm, ev_w_uk, ev_w_uv, ev_cmp_pos, ev_cmp_w1, ev_cmp_w2, ev_w_out, ev_norm_ffn, ev_w_gate, ev_w_up, ev_w_down, od_norm_mix, od_w_qkv, od_w_out, od_norm_ffn, od_w_router, od_w_gate, od_w_up, od_w_down, final_norm):
    b, s, d = x.shape
    depth = ev_norm_mix.shape[0] + od_norm_mix.shape[0]
    tokb, cmpb = _bias_tiles(rel_bias)
    x2d = x.reshape(b * s, d)
    for layer in range(depth):
        i = layer // 2
        last = layer == depth - 1
        if layer % 2 == 0:
            x2d = _even_mixer(x2d, b, s, ev_norm_mix[i], ev_w_in[i], ev_kv_norm[i], ev_w_uk[i], ev_w_uv[i],
                              ev_cmp_pos[i], ev_cmp_w1[i], ev_cmp_w2[i], ev_w_out[i], tokb, cmpb)
            x2d = _ffn(x2d, ev_norm_ffn[i], ev_w_gate[i][None], ev_w_up[i][None], ev_w_down[i][None],
                       final_g=final_norm if last else None, tf=ev_w_gate.shape[2] // 2)
        else:
            qkv = _rms_matmul(x2d, od_norm_mix[i], od_w_qkv[i])
            o = _stick(qkv.reshape(b, s, qkv.shape[1]), b, s)
            x2d = _proj_resid([o.reshape(b * s, o.shape[2])], [od_w_out[i]], x2d)
            x2d = _ffn(x2d, od_norm_ffn[i], od_w_gate[i], od_w_up[i], od_w_down[i], w_router=od_w_router[i],
                       final_g=final_norm if last else None, tf=od_w_gate.shape[3] // 4)
    return x2d.reshape(b, s, d)
```

```python
import functools
import math

import numpy as np
import jax
import jax.numpy as jnp
from jax import lax
from jax.experimental import pallas as pl
from jax.experimental.pallas import tpu as pltpu

HEAD_DIM = 128
A_HEADS = 4
A_KV_RANK = 256
IDX_HEADS = 4
IDX_DIM = 64
DSA_TOPK = 256
B_HEADS = 4
CMP_LEN = 32
CMP_STRIDE = 16
SEL_BLOCK = 64
N_SEL = 8
WINDOW = 512
C_HEADS = 8
NUM_BUCKETS = 32
MAX_DISTANCE = 128
N_EXPERTS = 8
RMS_EPS = 1e-6
NEG_BIG = -1e30

LANES = 128
CDT = jnp.bfloat16
F32 = jnp.float32
I32 = jnp.int32
INT_MIN = -2 ** 31
KEY_NEG_INF = -2 ** 31 + 0x7FFFFF
VMEM_LIMIT = 56 * 1024 * 1024

TQ = 256
KC = 256
SB_G = 128
SB_T = 512


def _cparams(sem):
    return pltpu.CompilerParams(dimension_semantics=sem, vmem_limit_bytes=VMEM_LIMIT)


def _dot(a, b):
    return jnp.dot(a, b, preferred_element_type=F32)


def _dot_nt(a, b):
    return lax.dot_general(a, b, (((1,), (1,)), ((), ())), preferred_element_type=F32)


def _rms(x, g):
    return x * lax.rsqrt(jnp.mean(x * x, axis=-1, keepdims=True) + RMS_EPS) * g


def _split_terms(x, n):
    if CDT == F32:
        return [x]
    out, r = [], x
    for _ in range(n):
        h = r.astype(CDT)
        out.append(h)
        r = r - h.astype(F32)
    return out


def _dot_split(x, w01, n):
    terms = _split_terms(x, n)
    return _dot(jnp.concatenate(terms, axis=1), jnp.concatenate([w01] * len(terms), axis=0))


def _sortable(v):
    bits = lax.bitcast_convert_type(v, I32)
    key = bits ^ ((bits >> 31) & 0x7FFFFFFF)
    return jnp.where(v == 0.0, 0, key)


def _stack_heads(x, n, w):
    return jnp.concatenate([x[:, h * w:(h + 1) * w] for h in range(n)], axis=0)


def _unstack_heads(x, n, t):
    return jnp.concatenate([x[h * t:(h + 1) * t] for h in range(n)], axis=1)


def _bucket_np(dist):
    n = np.maximum(dist, 0)
    max_exact = NUM_BUCKETS // 2
    large = max_exact + (np.log(np.maximum(n, 1).astype(np.float32) / np.float32(max_exact))
                         / np.float32(math.log(MAX_DISTANCE / max_exact))
                         * np.float32(NUM_BUCKETS - max_exact)).astype(np.int32)
    large = np.minimum(large, NUM_BUCKETS - 1)
    return np.where(n < max_exact, n, large).astype(np.int32)


def _bias_tile_kernel(bias_ref, tokbk_ref, cmpbk_ref, tok_ref, cmp_ref, *, n_heads):
    tb = tokbk_ref[...]
    cb = cmpbk_ref[...]
    for h in range(n_heads):
        far = bias_ref[NUM_BUCKETS - 1, h]
        tacc = jnp.zeros(tb.shape, F32)
        cacc = jnp.zeros(cb.shape, F32)
        for k in range(NUM_BUCKETS - 1):
            v = bias_ref[k, h] - far
            tacc = jnp.where(tb == k, v, tacc)
            cacc = jnp.where(cb == k, v, cacc)
        tok_ref[h * TQ:(h + 1) * TQ, :] = tacc
        cmp_ref[h * TQ:(h + 1) * TQ, :] = cacc


def _bias_tiles(rel_bias):
    n_heads = rel_bias.shape[1]
    i = np.arange(TQ)[:, None]
    j = np.arange(2 * KC)[None, :]
    dist = i + KC - j
    tokbk = np.where(dist >= 0, _bucket_np(dist), NUM_BUCKETS - 1).astype(np.int32)
    r = np.arange(LANES)[None, :]
    cdist = i + (TQ - CMP_LEN + 1) - CMP_STRIDE * r
    cmpbk = np.where((cdist >= 0) & (r < 2 * TQ // CMP_STRIDE), _bucket_np(cdist), NUM_BUCKETS - 1).astype(np.int32)
    return pl.pallas_call(
        functools.partial(_bias_tile_kernel, n_heads=n_heads),
        out_shape=(jax.ShapeDtypeStruct((n_heads * TQ, 2 * KC), F32),
                   jax.ShapeDtypeStruct((n_heads * TQ, LANES), F32)),
        in_specs=[pl.BlockSpec(memory_space=pltpu.SMEM),
                  pl.BlockSpec(memory_space=pltpu.VMEM),
                  pl.BlockSpec(memory_space=pltpu.VMEM)],
        out_specs=(pl.BlockSpec(memory_space=pltpu.VMEM), pl.BlockSpec(memory_space=pltpu.VMEM)),
        name="bias_tiles",
    )(rel_bias, jnp.asarray(tokbk), jnp.asarray(cmpbk))


EV_AQ = A_HEADS * HEAD_DIM
EV_IQ = IDX_HEADS * IDX_DIM
EV_BQ = B_HEADS * HEAD_DIM
EV_OFF_AQ = 0
EV_OFF_LAT = EV_OFF_AQ + EV_AQ
EV_OFF_IQ = EV_OFF_LAT + A_KV_RANK
EV_OFF_BQ = EV_OFF_IQ + EV_IQ
EV_OFF_KV = EV_OFF_BQ + EV_BQ
EV_OFF_SMALL = EV_OFF_KV + 6 * HEAD_DIM
EV_COLS = EV_OFF_SMALL + LANES
SM_IK = 0
SM_IW = IDX_DIM
SM_G = IDX_DIM + IDX_HEADS


def _even_inproj_kernel(x_ref, g_ref, w_ref, kvg_ref, wuk_ref, wuv_ref,
                        aq_ref, ak_ref, av_ref, iq_ref, ik_ref, bq_ref,
                        ckr_ref, cvr_ref, sk_ref, sv_ref, wk_ref, wv_ref, small_ref):
    h = _rms(x_ref[...], g_ref[...]).astype(CDT)

    def seg(off, width):
        return _dot(h, w_ref[:, off:off + width])

    aq_ref[...] = seg(EV_OFF_AQ, EV_AQ).astype(aq_ref.dtype)
    lat = seg(EV_OFF_LAT, A_KV_RANK)
    ckv = _rms(lat, kvg_ref[...]).astype(CDT)
    ak_ref[...] = _dot(ckv, wuk_ref[...]).astype(ak_ref.dtype)
    av_ref[...] = _dot(ckv, wuv_ref[...]).astype(av_ref.dtype)
    iq_ref[...] = seg(EV_OFF_IQ, EV_IQ).astype(iq_ref.dtype)
    bq_ref[...] = seg(EV_OFF_BQ, EV_BQ).astype(bq_ref.dtype)
    kv = seg(EV_OFF_KV, 6 * HEAD_DIM)
    for j, ref in enumerate((ckr_ref, cvr_ref, sk_ref, sv_ref, wk_ref, wv_ref)):
        ref[...] = kv[:, j * HEAD_DIM:(j + 1) * HEAD_DIM].astype(ref.dtype)
    small = seg(EV_OFF_SMALL, LANES)
    small_ref[...] = small
    ik_ref[...] = small[:, SM_IK:SM_IK + IDX_DIM].astype(ik_ref.dtype)


def _even_inproj(x2d, g, w_in, kv_norm, w_uk, w_uv, tm=512):
    m, d = x2d.shape
    offs = np.cumsum((EV_AQ, A_KV_RANK, EV_IQ, IDX_DIM, IDX_HEADS, EV_BQ, 6 * HEAD_DIM, 3 * B_HEADS))
    aq, lat, iq, ik, iw, bq, bkv, bg = jnp.split(w_in, [int(o) for o in offs[:-1]], axis=1)
    pad = jnp.zeros((d, LANES - IDX_DIM - IDX_HEADS - 3 * B_HEADS), w_in.dtype)
    w = jnp.concatenate([aq, lat, iq, bq, bkv, ik, iw, bg, pad], axis=1).astype(CDT)
    row = lambda width: pl.BlockSpec((tm, width), lambda i: (i, 0))
    full = lambda a: pl.BlockSpec(a.shape, lambda i: (0,) * a.ndim)
    g2, kvg2 = g.reshape(1, d), kv_norm.reshape(1, A_KV_RANK)
    wuk, wuv = w_uk.astype(CDT), w_uv.astype(CDT)
    widths = (EV_AQ, HEAD_DIM, HEAD_DIM, EV_IQ, IDX_DIM, EV_BQ) + (HEAD_DIM,) * 6 + (LANES,)
    dtypes = (CDT,) * 6 + (F32, F32) + (CDT,) * 4 + (F32,)
    return pl.pallas_call(
        _even_inproj_kernel,
        grid=(m // tm,),
        in_specs=[row(d), full(g2), full(w), full(kvg2), full(wuk), full(wuv)],
        out_specs=tuple(row(wd) for wd in widths),
        out_shape=tuple(jax.ShapeDtypeStruct((m, wd), dt) for wd, dt in zip(widths, dtypes)),
        compiler_params=_cparams(("parallel",)),
        name="even_inproj",
    )(x2d, g2, w, kvg2, wuk, wuv)


def _compress_kernel(kv_ref, pos_ref, w1_ref, w2_ref, o_ref):
    kv = kv_ref[0, 0]
    half = kv.shape[1]
    first = _dot((kv + pos_ref[0, :, :half]).astype(CDT), w1_ref[0, :half, :])
    second = _dot((kv + pos_ref[0, :, half:]).astype(CDT), w1_ref[0, half:, :])
    hid = first + pltpu.roll(second, second.shape[0] - 1, 0)
    hid = jax.nn.gelu(hid)
    o_ref[0, 0] = _dot(hid.astype(CDT), w2_ref[0]).astype(o_ref.dtype)


def _compress(raw, cmp_pos, cmp_w1, cmp_w2):
    _, b, s, hd = raw.shape
    g = s // CMP_STRIDE
    kv = raw.reshape(2, b, g, CMP_STRIDE * hd)
    pos = cmp_pos.reshape(2, 1, CMP_LEN * hd)
    return pl.pallas_call(
        _compress_kernel,
        grid=(2, b),
        in_specs=[pl.BlockSpec((1, 1, g, CMP_STRIDE * hd), lambda j, i: (j, i, 0, 0)),
                  pl.BlockSpec((1, 1, CMP_LEN * hd), lambda j, i: (j, 0, 0)),
                  pl.BlockSpec((1, CMP_LEN * hd, hd), lambda j, i: (j, 0, 0)),
                  pl.BlockSpec((1, hd, hd), lambda j, i: (j, 0, 0))],
        out_specs=pl.BlockSpec((1, 1, g, hd), lambda j, i: (j, i, 0, 0)),
        out_shape=jax.ShapeDtypeStruct((2, b, g, hd), CDT),
        compiler_params=_cparams(("parallel", "parallel")),
        name="nsa_compress",
    )(kv, pos, cmp_w1.astype(CDT), cmp_w2.astype(CDT))


def _softmax_step(carry, logits, mask4, v):
    m_i, l_i, acc = carry
    s = jnp.where(mask4, logits, NEG_BIG)
    m_new = jnp.maximum(m_i, jnp.max(s, axis=1, keepdims=True))
    alpha = jnp.exp(m_i - m_new)
    p = jnp.where(mask4, jnp.exp(s - m_new), 0.0)
    l_new = alpha * l_i + jnp.sum(p, axis=1, keepdims=True)
    acc = alpha * acc + _dot(p.astype(CDT), v)
    return m_new, l_new, acc


def _softmax_init(rows):
    return (jnp.full((rows, 1), NEG_BIG, F32), jnp.zeros((rows, 1), F32), jnp.zeros((rows, HEAD_DIM), F32))


def _softmax_out(carry):
    _, l_i, acc = carry
    return jnp.where(l_i > 0.0, acc / jnp.where(l_i > 0.0, l_i, 1.0), 0.0)


def _tile4(mask):
    return jnp.concatenate([mask] * 4, axis=0)


def _dsa_kernel(aq_ref, iq_ref, small_ref, ik_ref, ak_ref, av_ref, tokb_ref, o_ref, key_scr, j_scr,
                *, seq, topk):
    m = pl.program_id(1)
    scale = HEAD_DIM ** -0.5
    q_stack = _stack_heads(aq_ref[0], A_HEADS, HEAD_DIM)
    iq_stack = _stack_heads(iq_ref[0], IDX_HEADS, IDX_DIM)
    iw = small_ref[0][:, SM_IW:SM_IW + IDX_HEADS] * (IDX_HEADS ** -0.5)
    t_pos = m * TQ + lax.broadcasted_iota(I32, (TQ, 1), 0)
    lane = lax.broadcasted_iota(I32, (1, KC), 1)

    @pl.when(m == 0)
    def _():
        key_scr[...] = jnp.full(key_scr.shape, INT_MIN, I32)

    def chunk(c):
        return pl.ds(pl.multiple_of(c * KC, KC), KC)

    def score_body(c, carry):
        s = _dot_nt(iq_stack, ik_ref[0, chunk(c), :])
        isc = jnp.zeros((TQ, KC), F32)
        for h in range(IDX_HEADS):
            isc = isc + jnp.maximum(s[h * TQ:(h + 1) * TQ] * (IDX_DIM ** -0.5), 0.0) * iw[:, h:h + 1]
        isc = jnp.where(c * KC + lane <= t_pos, isc, -jnp.inf)
        key_scr[:, chunk(c)] = _sortable(isc)
        return carry

    lax.fori_loop(0, m + 1, score_body, 0)

    def count(pred):
        def body(c, acc):
            return acc + jnp.where(pred(key_scr[:, chunk(c)], c * KC + lane), 1.0, 0.0)
        acc = lax.fori_loop(0, m + 1, body, jnp.zeros((TQ, KC), F32))
        return jnp.sum(acc, axis=1, keepdims=True)

    kf = float(topk)
    t0 = jnp.where(count(lambda k, _: k >= 0) >= kf, 0, INT_MIN).astype(I32)

    def bit_body(i, t_cur):
        cand = t_cur | lax.shift_left(jnp.int32(1), 30 - i)
        return jnp.where(count(lambda k, _: k >= cand) >= kf, cand, t_cur)

    thr = lax.fori_loop(0, 31, bit_body, t0)
    n_ge = count(lambda k, _: k >= thr)
    n_gt = count(lambda k, _: k > thr)
    need = kf - n_gt
    tied = (n_ge > kf) & (thr > KEY_NEG_INF)
    j_scr[...] = jnp.full((TQ, 1), seq, I32)

    @pl.when(jnp.max(jnp.where(tied, 1.0, 0.0)) > 0.0)
    def _():
        def jbit(i, j_cur):
            cand = j_cur | lax.shift_left(jnp.int32(1), (seq.bit_length() - 2) - i)
            return jnp.where(count(lambda k, idx: (k == thr) & (idx < cand)) < need, cand, j_cur)
        j_last = lax.fori_loop(0, seq.bit_length() - 1, jbit, jnp.zeros((TQ, 1), I32))
        j_scr[...] = jnp.where(tied, j_last, seq)

    j_last = j_scr[...]

    def attend(c, carry, bias, causal):
        logits = _dot_nt(q_stack, ak_ref[0, chunk(c), :]) * scale
        if bias is not None:
            logits = logits + bias
        k = key_scr[:, chunk(c)]
        idx = c * KC + lane
        sel = (k > thr) | ((k == thr) & (idx <= j_last))
        if causal:
            sel = sel & (idx <= t_pos)
        return _softmax_step(carry, logits, _tile4(sel), av_ref[0, chunk(c), :])

    carry = _softmax_init(A_HEADS * TQ)
    carry = lax.fori_loop(0, jnp.maximum(m - 1, 0), lambda c, cr: attend(c, cr, None, False), carry)
    carry = lax.cond(m >= 1, lambda cr: attend(m - 1, cr, tokb_ref[:, :KC], False), lambda cr: cr, carry)
    carry = attend(m, carry, tokb_ref[:, KC:], True)
    o_ref[0] = _unstack_heads(_softmax_out(carry), A_HEADS, TQ).astype(o_ref.dtype)


def _dsa(aq, iq, small, ik, ak, av, tokb):
    b, s, _ = aq.shape
    topk = min(DSA_TOPK, s // 4)
    qspec = lambda w: pl.BlockSpec((1, TQ, w), lambda i, j: (i, j, 0))
    kspec = lambda w: pl.BlockSpec((1, s, w), lambda i, j: (i, 0, 0))
    return pl.pallas_call(
        functools.partial(_dsa_kernel, seq=s, topk=topk),
        grid=(b, s // TQ),
        in_specs=[qspec(EV_AQ), qspec(EV_IQ), qspec(LANES), kspec(IDX_DIM), kspec(HEAD_DIM), kspec(HEAD_DIM),
                  pl.BlockSpec((A_HEADS * TQ, 2 * KC), lambda i, j: (0, 0))],
        out_specs=qspec(EV_AQ),
        out_shape=jax.ShapeDtypeStruct((b, s, EV_AQ), CDT),
        scratch_shapes=[pltpu.VMEM((TQ, s), I32), pltpu.VMEM((TQ, 1), I32)],
        compiler_params=_cparams(("parallel", "arbitrary")),
        name="dsa_attention",
    )(aq, iq, small, ik, ak, av, tokb)


def _nsa_kernel(bq_ref, small_ref, ck_ref, cv_ref, sk_ref, sv_ref, wk_ref, wv_ref, tokb_ref, cmpb_ref, o_ref,
                *, seq):
    m = pl.program_id(1)
    scale = HEAD_DIM ** -0.5
    n_c = seq // CMP_STRIDE
    n_sb = seq // SEL_BLOCK
    q_stack = _stack_heads(bq_ref[0], B_HEADS, HEAD_DIM)
    t_pos = m * TQ + lax.broadcasted_iota(I32, (TQ, 1), 0)
    lane = lax.broadcasted_iota(I32, (1, KC), 1)

    def chunk(c):
        return pl.ds(pl.multiple_of(c * KC, KC), KC)

    lc = _dot_nt(q_stack, ck_ref[0]) * scale
    r_i = lax.broadcasted_iota(I32, (LANES, n_c), 0)
    n_i = lax.broadcasted_iota(I32, (LANES, n_c), 1)
    first_n = m * (TQ // CMP_STRIDE) - TQ // CMP_STRIDE
    place = jnp.where(n_i == first_n + r_i, 1.0, 0.0).astype(CDT)
    lc = lc + _dot_split(cmpb_ref[...], place, 2)
    n_row = lax.broadcasted_iota(I32, (1, n_c), 1)
    cvalid = (n_row * CMP_STRIDE + (CMP_LEN - 1) <= t_pos) & (n_row < n_c - 1)
    cv4 = _tile4(cvalid)
    sc = jnp.where(cv4, lc, NEG_BIG)
    pc = jnp.where(cv4, jnp.exp(sc - jnp.max(sc, axis=1, keepdims=True)), 0.0)
    den = jnp.sum(pc, axis=1, keepdims=True)
    pc = jnp.where(den > 0.0, pc / jnp.where(den > 0.0, den, 1.0), 0.0)
    oc = _dot(pc.astype(CDT), cv_ref[0])

    psum = pc[0:TQ]
    for h in range(1, B_HEADS):
        psum = psum + pc[h * TQ:(h + 1) * TQ]
    on = lax.broadcasted_iota(I32, (n_c, LANES), 0) * CMP_STRIDE
    om = lax.broadcasted_iota(I32, (n_c, LANES), 1) * SEL_BLOCK
    overlap = jnp.where((on < om + SEL_BLOCK) & (on + CMP_LEN > om) & (om < seq), 1.0, 0.0).astype(CDT)
    imp = _dot_split(psum, overlap, 3)
    blk = lax.broadcasted_iota(I32, (1, LANES), 1)
    cur = t_pos >> (SEL_BLOCK.bit_length() - 1)
    forced = (blk == 0) | (blk == cur) | (blk == cur - 1)
    val = jnp.where(forced, jnp.inf, jnp.where(blk > cur, -jnp.inf, imp))
    blk_f = blk.astype(F32)
    alive = blk < n_sb
    sel = jnp.zeros((TQ, LANES), F32)
    for _ in range(min(N_SEL, n_sb)):
        best = jnp.max(jnp.where(alive, val, -jnp.inf), axis=1, keepdims=True)
        first = jnp.min(jnp.where(alive & (val == best), blk_f, float(LANES)), axis=1, keepdims=True)
        pick = blk_f == first
        sel = jnp.where(pick, 1.0, sel)
        alive = alive & jnp.logical_not(pick)
    sel_c = sel.astype(CDT)

    def near_bias(near):
        return tokb_ref[:, near * KC:(near + 1) * KC]

    def sel_attend(c, carry, near):
        logits = _dot_nt(q_stack, sk_ref[0, chunk(c), :]) * scale
        e_blk = lax.broadcasted_iota(I32, (LANES, KC), 0)
        e_key = lax.broadcasted_iota(I32, (LANES, KC), 1)
        expand = jnp.where(e_blk == c * (KC // SEL_BLOCK) + (e_key >> (SEL_BLOCK.bit_length() - 1)), 1.0, 0.0).astype(CDT)
        mask = _dot(sel_c, expand) > 0.5
        if near is not None:
            logits = logits + near_bias(near)
            mask = mask & (c * KC + lane <= t_pos)
        return _softmax_step(carry, logits, _tile4(mask), sv_ref[0, chunk(c), :])

    cs = _softmax_init(B_HEADS * TQ)
    cs = lax.fori_loop(0, jnp.maximum(m - 1, 0), lambda c, cr: sel_attend(c, cr, None), cs)
    cs = lax.cond(m >= 1, lambda cr: sel_attend(m - 1, cr, 0), lambda cr: cr, cs)
    cs = sel_attend(m, cs, 1)
    osel = _softmax_out(cs)

    def win_attend(c, carry, near):
        logits = _dot_nt(q_stack, wk_ref[0, chunk(c), :]) * scale
        dist = t_pos - (c * KC + lane)
        mask = (dist >= 0) & (dist < WINDOW)
        if near is not None:
            logits = logits + near_bias(near)
        return _softmax_step(carry, logits, _tile4(mask), wv_ref[0, chunk(c), :])

    cw = _softmax_init(B_HEADS * TQ)
    for back in range(WINDOW // KC, 1, -1):
        cw = lax.cond(m >= back, lambda cr, back=back: win_attend(m - back, cr, None), lambda cr: cr, cw)
    cw = lax.cond(m >= 1, lambda cr: win_attend(m - 1, cr, 0), lambda cr: cr, cw)
    cw = win_attend(m, cw, 1)
    ow = _softmax_out(cw)

    gate = jax.nn.sigmoid(small_ref[0][:, SM_G:SM_G + 3 * B_HEADS])
    outs = []
    for h in range(B_HEADS):
        rows = slice(h * TQ, (h + 1) * TQ)
        outs.append(gate[:, h:h + 1] * oc[rows]
                    + gate[:, B_HEADS + h:B_HEADS + h + 1] * osel[rows]
                    + gate[:, 2 * B_HEADS + h:2 * B_HEADS + h + 1] * ow[rows])
    o_ref[0] = jnp.concatenate(outs, axis=1).astype(o_ref.dtype)


def _nsa(bq, small, ck, cv, sk, sv, wk, wv, tokb, cmpb):
    b, s, _ = bq.shape
    n_c = s // CMP_STRIDE
    qspec = lambda w: pl.BlockSpec((1, TQ, w), lambda i, j: (i, j, 0))
    kspec = lambda n: pl.BlockSpec((1, n, HEAD_DIM), lambda i, j: (i, 0, 0))
    return pl.pallas_call(
        functools.partial(_nsa_kernel, seq=s),
        grid=(b, s // TQ),
        in_specs=[qspec(EV_BQ), qspec(LANES), kspec(n_c), kspec(n_c), kspec(s), kspec(s), kspec(s), kspec(s),
                  pl.BlockSpec((B_HEADS * TQ, 2 * KC), lambda i, j: (1, 0)),
                  pl.BlockSpec((B_HEADS * TQ, LANES), lambda i, j: (1, 0))],
        out_specs=qspec(EV_BQ),
        out_shape=jax.ShapeDtypeStruct((b, s, EV_BQ), CDT),
        compiler_params=_cparams(("parallel", "arbitrary")),
        name="nsa_attention",
    )(bq, small, ck, cv, sk, sv, wk, wv, tokb, cmpb)


def _proj_resid_kernel(*refs, n_in):
    a_refs, w_refs, r_ref, o_ref = refs[:n_in], refs[n_in:2 * n_in], refs[2 * n_in], refs[2 * n_in + 1]
    acc = r_ref[...]
    for a_ref, w_ref in zip(a_refs, w_refs):
        acc = acc + _dot(a_ref[...], w_ref[...])
    o_ref[...] = acc


def _proj_resid(acts, weights, resid, tm=512):
    m, d = resid.shape
    row = lambda width: pl.BlockSpec((tm, width), lambda i: (i, 0))
    full = lambda a: pl.BlockSpec(a.shape, lambda i: (0, 0))
    weights = [w.astype(CDT) for w in weights]
    return pl.pallas_call(
        functools.partial(_proj_resid_kernel, n_in=len(acts)),
        grid=(m // tm,),
        in_specs=[row(a.shape[1]) for a in acts] + [full(w) for w in weights] + [row(d)],
        out_specs=row(d),
        out_shape=jax.ShapeDtypeStruct((m, d), F32),
        compiler_params=_cparams(("parallel",)),
        name="proj_residual",
    )(*acts, *weights, resid)


def _rms_matmul_kernel(x_ref, g_ref, w_ref, o_ref):
    h = _rms(x_ref[...], g_ref[...]).astype(CDT)
    o_ref[...] = _dot(h, w_ref[...]).astype(o_ref.dtype)


def _rms_matmul(x2d, g, w, tm=512, tn=1024):
    m, d = x2d.shape
    n = w.shape[1]
    return pl.pallas_call(
        _rms_matmul_kernel,
        grid=(m // tm, n // tn),
        in_specs=[pl.BlockSpec((tm, d), lambda i, j: (i, 0)),
                  pl.BlockSpec((1, d), lambda i, j: (0, 0)),
                  pl.BlockSpec((d, tn), lambda i, j: (0, j))],
        out_specs=pl.BlockSpec((tm, tn), lambda i, j: (i, j)),
        out_shape=jax.ShapeDtypeStruct((m, n), CDT),
        compiler_params=_cparams(("parallel", "parallel")),
        name="rms_matmul",
    )(x2d, g.reshape(1, d), w.astype(CDT))


def _stick_kernel(q_ref, k_ref, v_ref, o_ref):
    m = pl.program_id(2)
    scale = HEAD_DIM ** -0.5
    q = q_ref[0]
    per = SB_T // SB_G
    t_pos = m * SB_T + lax.broadcasted_iota(I32, (SB_T, 1), 0)
    lane = lax.broadcasted_iota(I32, (1, SB_T), 1)
    n_terms = 1 if CDT == F32 else 2
    uj = lax.broadcasted_iota(I32, (SB_G, 2 * SB_G), 0)
    us = lax.broadcasted_iota(I32, (SB_G, 2 * SB_G), 1)
    suffix = jnp.where((us >= SB_G) | (uj >= us), 1.0, 0.0).astype(CDT)

    def chunk(c, carry, masked):
        run, acc = carry
        rows = pl.ds(pl.multiple_of(c * SB_T, SB_T), SB_T)
        z = _dot_nt(q, k_ref[0, rows, :]) * scale
        log_1m = -(jnp.maximum(z, 0.0) + jnp.log(1.0 + jnp.exp(-jnp.abs(z))))
        if masked:
            strict = c * SB_T + lane < t_pos
            log_1m = jnp.where(strict, log_1m, 0.0)
        stacked = jnp.concatenate([log_1m[:, g * SB_G:(g + 1) * SB_G] for g in range(per)], axis=0)
        sums = _dot_split(stacked, suffix, n_terms)
        later = [None] * per
        for g in range(per - 1, -1, -1):
            later[g] = run + sums[g * SB_T:(g + 1) * SB_T, :SB_G]
            run = run + sums[g * SB_T:(g + 1) * SB_T, SB_G:]
        a = jnp.exp(z + jnp.concatenate(later, axis=1))
        if masked:
            a = jnp.where(strict, a, 0.0)
        acc = acc + _dot(a.astype(CDT), v_ref[0, rows, :])
        return run, acc

    carry = (jnp.zeros((SB_T, SB_G), F32), jnp.zeros((SB_T, HEAD_DIM), F32))
    carry = chunk(m, carry, True)
    carry = lax.fori_loop(0, m, lambda i, cr: chunk(m - 1 - i, cr, False), carry)
    o_ref[0] = carry[1].astype(o_ref.dtype)


def _stick(qkv, b, s):
    h = C_HEADS
    return pl.pallas_call(
        _stick_kernel,
        grid=(b, h, s // SB_T),
        in_specs=[pl.BlockSpec((1, SB_T, HEAD_DIM), lambda i, j, k: (i, k, j)),
                  pl.BlockSpec((1, s, HEAD_DIM), lambda i, j, k: (i, 0, h + j)),
                  pl.BlockSpec((1, s, HEAD_DIM), lambda i, j, k: (i, 0, 2 * h + j))],
        out_specs=pl.BlockSpec((1, SB_T, HEAD_DIM), lambda i, j, k: (i, k, j)),
        out_shape=jax.ShapeDtypeStruct((b, s, h * HEAD_DIM), CDT),
        compiler_params=_cparams(("parallel", "parallel", "arbitrary")),
        name="stick_breaking",
    )(qkv, qkv, qkv)


def _ffn_kernel(*refs, routed, final):
    it = iter(refs)
    x_ref, g_ref = next(it), next(it)
    wr_ref = next(it) if routed else None
    wg_ref, wu_ref, wd_ref = next(it), next(it), next(it)
    fg_ref = next(it) if final else None
    o_ref, h_scr, acc_scr = next(it), next(it), next(it)
    if routed:
        gate_scr, eacc_scr = next(it), next(it)
    e, f = pl.program_id(1), pl.program_id(2)
    n_e, n_f = pl.num_programs(1), pl.num_programs(2)

    @pl.when((e == 0) & (f == 0))
    def _():
        hn = _rms(x_ref[...], g_ref[...])
        h_scr[...] = hn.astype(CDT)
        acc_scr[...] = jnp.zeros(acc_scr.shape, F32)
        if routed:
            logits = jnp.dot(hn, wr_ref[...], preferred_element_type=F32, precision=lax.Precision.HIGHEST)
            lane = lax.broadcasted_iota(I32, logits.shape, 1)
            logits = jnp.where(lane < N_EXPERTS, logits, -jnp.inf)
            v1 = jnp.max(logits, axis=1, keepdims=True)
            i1 = jnp.min(jnp.where(logits == v1, lane, LANES), axis=1, keepdims=True)
            rest = jnp.where(lane == i1, -jnp.inf, logits)
            v2 = jnp.max(rest, axis=1, keepdims=True)
            i2 = jnp.min(jnp.where(rest == v2, lane, LANES), axis=1, keepdims=True)
            e2 = jnp.exp(v2 - v1)
            den = 1.0 + e2
            gate_scr[...] = jnp.where(lane == i1, 1.0 / den, 0.0) + jnp.where(lane == i2, e2 / den, 0.0)

    h = h_scr[...]
    act = jax.nn.silu(_dot(h, wg_ref[0])) * _dot(h, wu_ref[0])
    y = _dot(act.astype(CDT), wd_ref[0])
    if routed:
        @pl.when(f == 0)
        def _():
            eacc_scr[...] = y

        @pl.when(f > 0)
        def _():
            eacc_scr[...] += y

        @pl.when(f == n_f - 1)
        def _():
            lane = lax.broadcasted_iota(I32, gate_scr.shape, 1)
            ge = jnp.sum(jnp.where(lane == e, gate_scr[...], 0.0), axis=1, keepdims=True)
            acc_scr[...] += ge * eacc_scr[...]
    else:
        acc_scr[...] += y

    @pl.when((e == n_e - 1) & (f == n_f - 1))
    def _():
        out = x_ref[...] + acc_scr[...]
        if final:
            out = _rms(out, fg_ref[...])
        o_ref[...] = out


def _ffn(x2d, g, w_gate, w_up, w_down, w_router=None, final_g=None, tm=512, tf=None):
    m, d = x2d.shape
    n_e, _, ff = w_gate.shape
    routed, final = w_router is not None, final_g is not None
    row = pl.BlockSpec((tm, d), lambda i, e, f: (i, 0))
    vec = pl.BlockSpec((1, d), lambda i, e, f: (0, 0))
    ins, specs = [x2d, g.reshape(1, d)], [row, vec]
    if routed:
        wr = jnp.concatenate([w_router, jnp.zeros((d, LANES - n_e), w_router.dtype)], axis=1)
        ins.append(wr)
        specs.append(pl.BlockSpec((d, LANES), lambda i, e, f: (0, 0)))
    ins += [w_gate.astype(CDT), w_up.astype(CDT), w_down.astype(CDT)]
    specs += [pl.BlockSpec((1, d, tf), lambda i, e, f: (e, 0, f)),
              pl.BlockSpec((1, d, tf), lambda i, e, f: (e, 0, f)),
              pl.BlockSpec((1, tf, d), lambda i, e, f: (e, f, 0))]
    if final:
        ins.append(final_g.reshape(1, d))
        specs.append(vec)
    scratch = [pltpu.VMEM((tm, d), CDT), pltpu.VMEM((tm, d), F32)]
    if routed:
        scratch += [pltpu.VMEM((tm, LANES), F32), pltpu.VMEM((tm, d), F32)]
    return pl.pallas_call(
        functools.partial(_ffn_kernel, routed=routed, final=final),
        grid=(m // tm, n_e, ff // tf),
        in_specs=specs,
        out_specs=row,
        out_shape=jax.ShapeDtypeStruct((m, d), F32),
        scratch_shapes=scratch,
        compiler_params=_cparams(("parallel", "arbitrary", "arbitrary")),
        name="routed_swiglu" if routed else "swiglu",
    )(*ins)


def _even_mixer(x2d, b, s, g, w_in, kv_norm, w_uk, w_uv, cmp_pos, cmp_w1, cmp_w2, w_out, tokb, cmpb):
    (aq, ak, av, iq, ik, bq, ckr, cvr, sk, sv, wk, wv, small) = _even_inproj(x2d, g, w_in, kv_norm, w_uk, w_uv)
    r3 = lambda a: a.reshape(b, s, a.shape[1])
    oa = _dsa(r3(aq), r3(iq), r3(small), r3(ik), r3(ak), r3(av), tokb)
    cmp = _compress(jnp.stack([ckr, cvr]).reshape(2, b, s, HEAD_DIM), cmp_pos, cmp_w1, cmp_w2)
    ob = _nsa(r3(bq), r3(small), cmp[0], cmp[1], r3(sk), r3(sv), r3(wk), r3(wv), tokb, cmpb)
    return _proj_resid([oa.reshape(b * s, EV_AQ), ob.reshape(b * s, EV_BQ)],
                       [w_out[:EV_AQ], w_out[EV_AQ:]], x2d)


def kernel(x, rel_bias, ev_norm_mix, ev_w_in, ev_kv_norm, ev_w_uk, ev_w_uv, ev_cmp_pos, ev_cmp_w1, ev_cmp_w2, ev_w_out, ev_norm_ffn, ev_w_gate, ev_w_up, ev_w_down, od_norm_mix, od_w_qkv, od_w_out, od_norm_ffn, od_w_router, od_w_gate, od_w_up, od_w_down, final_norm):
    b, s, d = x.shape
    depth = ev_norm_mix.shape[0] + od_norm_mix.shape[0]
    tokb, cmpb = _bias_tiles(rel_bias)
    x2d = x.reshape(b * s, d)
    for layer in range(depth):
        i = layer // 2
        last = layer == depth - 1
        if layer % 2 == 0:
            x2d = _even_mixer(x2d, b, s, ev_norm_mix[i], ev_w_in[i], ev_kv_norm[i], ev_w_uk[i], ev_w_uv[i],
                              ev_cmp_pos[i], ev_cmp_w1[i], ev_cmp_w2[i], ev_w_out[i], tokb, cmpb)
            x2d = _ffn(x2d, ev_norm_ffn[i], ev_w_gate[i][None], ev_w_up[i][None], ev_w_down[i][None],
                       final_g=final_norm if last else None, tf=ev_w_gate.shape[2] // 2)
        else:
            qkv = _rms_matmul(x2d, od_norm_mix[i], od_w_qkv[i])
            o = _stick(qkv.reshape(b, s, qkv.shape[1]), b, s)
            x2d = _proj_resid([o.reshape(b * s, o.shape[2])], [od_w_out[i]], x2d)
            x2d = _ffn(x2d, od_norm_ffn[i], od_w_gate[i], od_w_up[i], od_w_down[i], w_router=od_w_router[i],
                       final_g=final_norm if last else None, tf=od_w_gate.shape[3] // 4)
    return x2d.reshape(b, s, d)
```

```python
import functools
import math

import numpy as np
import jax
import jax.numpy as jnp
from jax import lax
from jax.experimental import pallas as pl
from jax.experimental.pallas import tpu as pltpu

HEAD_DIM = 128
A_HEADS = 4
A_KV_RANK = 256
IDX_HEADS = 4
IDX_DIM = 64
DSA_TOPK = 256
B_HEADS = 4
CMP_LEN = 32
CMP_STRIDE = 16
SEL_BLOCK = 64
N_SEL = 8
WINDOW = 512
C_HEADS = 8
NUM_BUCKETS = 32
MAX_DISTANCE = 128
N_EXPERTS = 8
RMS_EPS = 1e-6
NEG_BIG = -1e30

LANES = 128
CDT = jnp.bfloat16
F32 = jnp.float32
I32 = jnp.int32
INT_MIN = -2 ** 31
KEY_NEG_INF = -2 ** 31 + 0x7FFFFF
VMEM_LIMIT = 56 * 1024 * 1024

TQ = 256
KC = 256
SB_G = 128
SB_T = 512


def _cparams(sem):
    return pltpu.CompilerParams(dimension_semantics=sem, vmem_limit_bytes=VMEM_LIMIT)


def _dot(a, b):
    return jnp.dot(a, b, preferred_element_type=F32)


def _dot_nt(a, b):
    return lax.dot_general(a, b, (((1,), (1,)), ((), ())), preferred_element_type=F32)


def _rms(x, g):
    return x * lax.rsqrt(jnp.mean(x * x, axis=-1, keepdims=True) + RMS_EPS) * g


def _split_terms(x, n):
    if CDT == F32:
        return [x]
    out, r = [], x
    for _ in range(n):
        h = r.astype(CDT)
        out.append(h)
        r = r - h.astype(F32)
    return out


def _dot_split(x, w01, n):
    terms = _split_terms(x, n)
    return _dot(jnp.concatenate(terms, axis=1), jnp.concatenate([w01] * len(terms), axis=0))


def _sortable(v):
    bits = lax.bitcast_convert_type(v, I32)
    key = bits ^ ((bits >> 31) & 0x7FFFFFFF)
    return jnp.where(v == 0.0, 0, key)


def _stack_heads(x, n, w):
    return jnp.concatenate([x[:, h * w:(h + 1) * w] for h in range(n)], axis=0)


def _unstack_heads(x, n, t):
    return jnp.concatenate([x[h * t:(h + 1) * t] for h in range(n)], axis=1)


def _bucket_np(dist):
    n = np.maximum(dist, 0)
    max_exact = NUM_BUCKETS // 2
    large = max_exact + (np.log(np.maximum(n, 1).astype(np.float32) / np.float32(max_exact))
                         / np.float32(math.log(MAX_DISTANCE / max_exact))
                         * np.float32(NUM_BUCKETS - max_exact)).astype(np.int32)
    large = np.minimum(large, NUM_BUCKETS - 1)
    return np.where(n < max_exact, n, large).astype(np.int32)


def _bias_tile_kernel(bias_ref, tokbk_ref, tokbkt_ref, cmpbk_ref, tok_ref, tokt_ref, cmp_ref, *, n_heads):
    tb = tokbk_ref[...]
    tbt = tokbkt_ref[...]
    cb = cmpbk_ref[...]
    for h in range(n_heads):
        far = bias_ref[NUM_BUCKETS - 1, h]
        tacc = jnp.zeros(tb.shape, F32)
        ttacc = jnp.zeros(tbt.shape, F32)
        cacc = jnp.zeros(cb.shape, F32)
        for k in range(NUM_BUCKETS - 1):
            v = bias_ref[k, h] - far
            tacc = jnp.where(tb == k, v, tacc)
            ttacc = jnp.where(tbt == k, v, ttacc)
            cacc = jnp.where(cb == k, v, cacc)
        tok_ref[h * TQ:(h + 1) * TQ, :] = tacc
        tokt_ref[:, h * TQ:(h + 1) * TQ] = ttacc
        cmp_ref[h * TQ:(h + 1) * TQ, :] = cacc


def _bias_tiles(rel_bias):
    n_heads = rel_bias.shape[1]
    i = np.arange(TQ)[:, None]
    j = np.arange(2 * KC)[None, :]
    dist = i + KC - j
    tokbk = np.where(dist >= 0, _bucket_np(dist), NUM_BUCKETS - 1).astype(np.int32)
    r = np.arange(LANES)[None, :]
    cdist = i + (TQ - CMP_LEN + 1) - CMP_STRIDE * r
    cmpbk = np.where((cdist >= 0) & (r < 2 * TQ // CMP_STRIDE), _bucket_np(cdist), NUM_BUCKETS - 1).astype(np.int32)
    return pl.pallas_call(
        functools.partial(_bias_tile_kernel, n_heads=n_heads),
        out_shape=(jax.ShapeDtypeStruct((n_heads * TQ, 2 * KC), F32),
                   jax.ShapeDtypeStruct((2 * KC, n_heads * TQ), F32),
                   jax.ShapeDtypeStruct((n_heads * TQ, LANES), F32)),
        in_specs=[pl.BlockSpec(memory_space=pltpu.SMEM)] + [pl.BlockSpec(memory_space=pltpu.VMEM)] * 3,
        out_specs=(pl.BlockSpec(memory_space=pltpu.VMEM),) * 3,
        compiler_params=pltpu.CompilerParams(vmem_limit_bytes=VMEM_LIMIT),
        name="bias_tiles",
    )(rel_bias, jnp.asarray(tokbk), jnp.asarray(tokbk.T.copy()), jnp.asarray(cmpbk))


EV_AQ = A_HEADS * HEAD_DIM
EV_IQ = IDX_HEADS * IDX_DIM
EV_BQ = B_HEADS * HEAD_DIM
EV_OFF_LAT = 0
EV_OFF_BQ = EV_OFF_LAT + A_KV_RANK
EV_OFF_KV = EV_OFF_BQ + EV_BQ
EV_OFF_SMALL = EV_OFF_KV + 6 * HEAD_DIM
EV_TOFF_AQ = 0
EV_TOFF_IQ = EV_TOFF_AQ + EV_AQ
EV_TOFF_SMALL = EV_TOFF_IQ + EV_IQ
SM_IK = 0
SM_IW = IDX_DIM
SM_G = IDX_DIM + IDX_HEADS


def _even_inproj_kernel(x_ref, g_ref, w_ref, wt_ref, kvg_ref, wuk_ref, wuvt_ref,
                        aqt_ref, iqt_ref, smallt_ref, ak_ref, avt_ref, ik_ref, bq_ref,
                        ckr_ref, cvr_ref, sk_ref, sv_ref, wk_ref, wv_ref, small_ref):
    h = _rms(x_ref[...], g_ref[...]).astype(CDT)

    def seg(off, width):
        return _dot(h, w_ref[:, off:off + width])

    def seg_t(off, width):
        return _dot_nt(wt_ref[off:off + width, :], h)

    aqt_ref[...] = seg_t(EV_TOFF_AQ, EV_AQ).astype(aqt_ref.dtype)
    iqt_ref[...] = seg_t(EV_TOFF_IQ, EV_IQ).astype(iqt_ref.dtype)
    smallt_ref[...] = seg_t(EV_TOFF_SMALL, LANES)
    lat = seg(EV_OFF_LAT, A_KV_RANK)
    ckv = _rms(lat, kvg_ref[...]).astype(CDT)
    ak_ref[...] = _dot(ckv, wuk_ref[...]).astype(ak_ref.dtype)
    avt_ref[...] = _dot_nt(wuvt_ref[...], ckv).astype(avt_ref.dtype)
    bq_ref[...] = seg(EV_OFF_BQ, EV_BQ).astype(bq_ref.dtype)
    kv = seg(EV_OFF_KV, 6 * HEAD_DIM)
    for j, ref in enumerate((ckr_ref, cvr_ref, sk_ref, sv_ref, wk_ref, wv_ref)):
        ref[...] = kv[:, j * HEAD_DIM:(j + 1) * HEAD_DIM].astype(ref.dtype)
    small = seg(EV_OFF_SMALL, LANES)
    small_ref[...] = small
    ik_ref[...] = small[:, SM_IK:SM_IK + IDX_DIM].astype(ik_ref.dtype)


def _even_inproj(x2d, g, w_in, kv_norm, w_uk, w_uv, tm=512):
    m, d = x2d.shape
    offs = np.cumsum((EV_AQ, A_KV_RANK, EV_IQ, IDX_DIM, IDX_HEADS, EV_BQ, 6 * HEAD_DIM, 3 * B_HEADS))
    aq, lat, iq, ik, iw, bq, bkv, bg = jnp.split(w_in, [int(o) for o in offs[:-1]], axis=1)
    pad = jnp.zeros((d, LANES - IDX_DIM - IDX_HEADS - 3 * B_HEADS), w_in.dtype)
    small_w = jnp.concatenate([ik, iw, bg, pad], axis=1)
    w = jnp.concatenate([lat, bq, bkv, small_w], axis=1).astype(CDT)
    wt = jnp.concatenate([aq, iq, small_w], axis=1).T.astype(CDT)
    row = lambda width: pl.BlockSpec((tm, width), lambda i: (i, 0))
    col = lambda height: pl.BlockSpec((height, tm), lambda i: (0, i))
    full = lambda a: pl.BlockSpec(a.shape, lambda i: (0,) * a.ndim)
    g2, kvg2 = g.reshape(1, d), kv_norm.reshape(1, A_KV_RANK)
    wuk, wuvt = w_uk.astype(CDT), w_uv.T.astype(CDT)
    heights = (EV_AQ, EV_IQ, LANES)
    widths = (HEAD_DIM, None, IDX_DIM, EV_BQ) + (HEAD_DIM,) * 6 + (LANES,)
    dtypes = (CDT,) * 4 + (F32, F32) + (CDT,) * 4 + (F32,)
    out_specs = [col(ht) for ht in heights] + [col(HEAD_DIM) if wd is None else row(wd) for wd in widths]
    out_shape = ([jax.ShapeDtypeStruct((ht, m), dt) for ht, dt in zip(heights, (CDT, CDT, F32))]
                 + [jax.ShapeDtypeStruct((HEAD_DIM, m) if wd is None else (m, wd), dt)
                    for wd, dt in zip(widths, dtypes)])
    return pl.pallas_call(
        _even_inproj_kernel,
        grid=(m // tm,),
        in_specs=[row(d), full(g2), full(w), full(wt), full(kvg2), full(wuk), full(wuvt)],
        out_specs=tuple(out_specs),
        out_shape=tuple(out_shape),
        compiler_params=_cparams(("parallel",)),
        name="even_inproj",
    )(x2d, g2, w, wt, kvg2, wuk, wuvt)


def _compress_kernel(kv_ref, pos_ref, w1_ref, w2_ref, o_ref):
    kv = kv_ref[0, 0]
    half = kv.shape[1]
    first = _dot((kv + pos_ref[0, :, :half]).astype(CDT), w1_ref[0, :half, :])
    second = _dot((kv + pos_ref[0, :, half:]).astype(CDT), w1_ref[0, half:, :])
    hid = first + pltpu.roll(second, second.shape[0] - 1, 0)
    hid = jax.nn.gelu(hid)
    o_ref[0, 0] = _dot(hid.astype(CDT), w2_ref[0]).astype(o_ref.dtype)


def _compress(raw, cmp_pos, cmp_w1, cmp_w2):
    _, b, s, hd = raw.shape
    g = s // CMP_STRIDE
    kv = raw.reshape(2, b, g, CMP_STRIDE * hd)
    pos = cmp_pos.reshape(2, 1, CMP_LEN * hd)
    return pl.pallas_call(
        _compress_kernel,
        grid=(2, b),
        in_specs=[pl.BlockSpec((1, 1, g, CMP_STRIDE * hd), lambda j, i: (j, i, 0, 0)),
                  pl.BlockSpec((1, 1, CMP_LEN * hd), lambda j, i: (j, 0, 0)),
                  pl.BlockSpec((1, CMP_LEN * hd, hd), lambda j, i: (j, 0, 0)),
                  pl.BlockSpec((1, hd, hd), lambda j, i: (j, 0, 0))],
        out_specs=pl.BlockSpec((1, 1, g, hd), lambda j, i: (j, i, 0, 0)),
        out_shape=jax.ShapeDtypeStruct((2, b, g, hd), CDT),
        compiler_params=_cparams(("parallel", "parallel")),
        name="nsa_compress",
    )(kv, pos, cmp_w1.astype(CDT), cmp_w2.astype(CDT))


def _softmax_step(carry, logits, mask4, v):
    m_i, l_i, acc = carry
    s = jnp.where(mask4, logits, NEG_BIG)
    m_new = jnp.maximum(m_i, jnp.max(s, axis=1, keepdims=True))
    alpha = jnp.exp(m_i - m_new)
    p = jnp.where(mask4, jnp.exp(s - m_new), 0.0)
    l_new = alpha * l_i + jnp.sum(p, axis=1, keepdims=True)
    acc = alpha * acc + _dot(p.astype(CDT), v)
    return m_new, l_new, acc


def _softmax_init(rows):
    return (jnp.full((rows, 1), NEG_BIG, F32), jnp.zeros((rows, 1), F32), jnp.zeros((rows, HEAD_DIM), F32))


def _softmax_out(carry):
    _, l_i, acc = carry
    return jnp.where(l_i > 0.0, acc / jnp.where(l_i > 0.0, l_i, 1.0), 0.0)


def _tile4(mask):
    return jnp.concatenate([mask] * 4, axis=0)


def _dsa_kernel(aqt_ref, iqt_ref, smallt_ref, ik_ref, ak_ref, avt_ref, tokbt_ref, o_ref, key_scr, j_scr,
                *, seq, topk):
    m = pl.program_id(1)
    scale = HEAD_DIM ** -0.5
    lanes4 = A_HEADS * TQ
    q_t = jnp.concatenate([aqt_ref[h * HEAD_DIM:(h + 1) * HEAD_DIM, :] for h in range(A_HEADS)], axis=1)
    iq_t = jnp.concatenate([iqt_ref[h * IDX_DIM:(h + 1) * IDX_DIM, :] for h in range(IDX_HEADS)], axis=1)
    iw = smallt_ref[SM_IW:SM_IW + IDX_HEADS, :] * (IDX_HEADS ** -0.5)
    t_pos = m * TQ + lax.broadcasted_iota(I32, (1, TQ), 1)
    kidx = lax.broadcasted_iota(I32, (KC, 1), 0)

    @pl.when(m == 0)
    def _():
        key_scr[...] = jnp.full(key_scr.shape, INT_MIN, I32)

    def chunk(c):
        return pl.ds(pl.multiple_of(c * KC, KC), KC)

    def score_body(c, carry):
        s = _dot(ik_ref[0, chunk(c), :], iq_t)
        isc = jnp.zeros((KC, TQ), F32)
        for h in range(IDX_HEADS):
            isc = isc + jnp.maximum(s[:, h * TQ:(h + 1) * TQ] * (IDX_DIM ** -0.5), 0.0) * iw[h:h + 1, :]
        isc = jnp.where(c * KC + kidx <= t_pos, isc, -jnp.inf)
        key_scr[chunk(c), :] = _sortable(isc)
        return carry

    lax.fori_loop(0, m + 1, score_body, 0)

    def count(pred):
        def body(c, acc):
            hit = jnp.where(pred(key_scr[chunk(c), :], c * KC + kidx), 1.0, 0.0)
            return acc + jnp.sum(hit.reshape(KC // 8, 8, TQ), axis=0)
        acc = lax.fori_loop(0, m + 1, body, jnp.zeros((8, TQ), F32))
        return jnp.sum(acc, axis=0, keepdims=True)

    kf = float(topk)
    t0 = jnp.where(count(lambda k, _: k >= 0) >= kf, 0, INT_MIN).astype(I32)

    def bit_body(i, t_cur):
        cand = t_cur | lax.shift_left(jnp.int32(1), 30 - i)
        return jnp.where(count(lambda k, _: k >= cand) >= kf, cand, t_cur)

    thr = lax.fori_loop(0, 31, bit_body, t0)
    n_ge = count(lambda k, _: k >= thr)
    n_gt = count(lambda k, _: k > thr)
    need = kf - n_gt
    tied = (n_ge > kf) & (thr > KEY_NEG_INF)
    j_scr[...] = jnp.full(j_scr.shape, seq, I32)

    @pl.when(jnp.max(jnp.where(tied, 1.0, 0.0)) > 0.0)
    def _():
        def jbit(i, j_cur):
            cand = j_cur | lax.shift_left(jnp.int32(1), (seq.bit_length() - 2) - i)
            return jnp.where(count(lambda k, idx: (k == thr) & (idx < cand)) < need, cand, j_cur)
        j_found = lax.fori_loop(0, seq.bit_length() - 1, jbit, jnp.zeros((1, TQ), I32))
        j_scr[...] = jnp.broadcast_to(jnp.where(tied, j_found, seq), j_scr.shape)

    j_last = j_scr[0:1, :]

    def attend(c, carry, bias, causal):
        m_i, l_i, acc = carry
        logits = _dot(ak_ref[0, chunk(c), :], q_t) * scale
        if bias is not None:
            logits = logits + bias
        k = key_scr[chunk(c), :]
        idx = c * KC + kidx
        sel = (k > thr) | ((k == thr) & (idx <= j_last))
        if causal:
            sel = sel & (idx <= t_pos)
        sel4 = jnp.concatenate([sel] * A_HEADS, axis=1)
        s = jnp.where(sel4, logits, NEG_BIG)
        m_new = jnp.maximum(m_i, jnp.max(s, axis=0, keepdims=True))
        alpha = jnp.exp(m_i - m_new)
        p = jnp.where(sel4, jnp.exp(s - m_new), 0.0)
        l_new = alpha * l_i + jnp.sum(p, axis=0, keepdims=True)
        acc = alpha * acc + _dot(avt_ref[:, chunk(c)], p.astype(CDT))
        return m_new, l_new, acc

    carry = (jnp.full((1, lanes4), NEG_BIG, F32), jnp.zeros((1, lanes4), F32), jnp.zeros((HEAD_DIM, lanes4), F32))
    carry = lax.fori_loop(0, jnp.maximum(m - 1, 0), lambda c, cr: attend(c, cr, None, False), carry)
    carry = lax.cond(m >= 1, lambda cr: attend(m - 1, cr, tokbt_ref[:KC, :], False), lambda cr: cr, carry)
    _, l_i, acc = attend(m, carry, tokbt_ref[KC:, :], True)
    o_t = jnp.where(l_i > 0.0, acc / jnp.where(l_i > 0.0, l_i, 1.0), 0.0)
    o_ref[0] = jnp.concatenate([o_t[:, h * TQ:(h + 1) * TQ].T for h in range(A_HEADS)], axis=1).astype(o_ref.dtype)


def _dsa(aqt, iqt, smallt, ik, ak, avt, tokbt):
    b, s, _ = ak.shape
    nq = s // TQ
    topk = min(DSA_TOPK, s // 4)
    qspec = lambda rows: pl.BlockSpec((rows, TQ), lambda i, j: (0, i * nq + j))
    kspec = lambda w: pl.BlockSpec((1, s, w), lambda i, j: (i, 0, 0))
    return pl.pallas_call(
        functools.partial(_dsa_kernel, seq=s, topk=topk),
        grid=(b, nq),
        in_specs=[qspec(EV_AQ), qspec(EV_IQ), qspec(LANES), kspec(IDX_DIM), kspec(HEAD_DIM),
                  pl.BlockSpec((HEAD_DIM, s), lambda i, j: (0, i)),
                  pl.BlockSpec((2 * KC, A_HEADS * TQ), lambda i, j: (0, 0))],
        out_specs=pl.BlockSpec((1, TQ, EV_AQ), lambda i, j: (i, j, 0)),
        out_shape=jax.ShapeDtypeStruct((b, s, EV_AQ), CDT),
        scratch_shapes=[pltpu.VMEM((s, TQ), I32), pltpu.VMEM((8, TQ), I32)],
        compiler_params=_cparams(("parallel", "arbitrary")),
        name="dsa_attention",
    )(aqt, iqt, smallt, ik, ak, avt, tokbt)


def _nsa_kernel(bq_ref, small_ref, ck_ref, cv_ref, sk_ref, sv_ref, wk_ref, wv_ref, tokb_ref, cmpb_ref, o_ref,
                *, seq):
    m = pl.program_id(1)
    scale = HEAD_DIM ** -0.5
    n_c = seq // CMP_STRIDE
    n_sb = seq // SEL_BLOCK
    q_stack = _stack_heads(bq_ref[0], B_HEADS, HEAD_DIM)
    t_pos = m * TQ + lax.broadcasted_iota(I32, (TQ, 1), 0)
    lane = lax.broadcasted_iota(I32, (1, KC), 1)

    def chunk(c):
        return pl.ds(pl.multiple_of(c * KC, KC), KC)

    lc = _dot_nt(q_stack, ck_ref[0]) * scale
    r_i = lax.broadcasted_iota(I32, (LANES, n_c), 0)
    n_i = lax.broadcasted_iota(I32, (LANES, n_c), 1)
    first_n = m * (TQ // CMP_STRIDE) - TQ // CMP_STRIDE
    place = jnp.where(n_i == first_n + r_i, 1.0, 0.0).astype(CDT)
    lc = lc + _dot_split(cmpb_ref[...], place, 2)
    n_row = lax.broadcasted_iota(I32, (1, n_c), 1)
    cvalid = (n_row * CMP_STRIDE + (CMP_LEN - 1) <= t_pos) & (n_row < n_c - 1)
    cv4 = _tile4(cvalid)
    sc = jnp.where(cv4, lc, NEG_BIG)
    pc = jnp.where(cv4, jnp.exp(sc - jnp.max(sc, axis=1, keepdims=True)), 0.0)
    den = jnp.sum(pc, axis=1, keepdims=True)
    pc = jnp.where(den > 0.0, pc / jnp.where(den > 0.0, den, 1.0), 0.0)
    oc = _dot(pc.astype(CDT), cv_ref[0])

    psum = pc[0:TQ]
    for h in range(1, B_HEADS):
        psum = psum + pc[h * TQ:(h + 1) * TQ]
    on = lax.broadcasted_iota(I32, (n_c, LANES), 0) * CMP_STRIDE
    om = lax.broadcasted_iota(I32, (n_c, LANES), 1) * SEL_BLOCK
    overlap = jnp.where((on < om + SEL_BLOCK) & (on + CMP_LEN > om) & (om < seq), 1.0, 0.0).astype(CDT)
    imp = _dot_split(psum, overlap, 3)
    blk = lax.broadcasted_iota(I32, (1, LANES), 1)
    cur = t_pos >> (SEL_BLOCK.bit_length() - 1)
    forced = (blk == 0) | (blk == cur) | (blk == cur - 1)
    val = jnp.where(forced, jnp.inf, jnp.where(blk > cur, -jnp.inf, imp))
    blk_f = blk.astype(F32)
    alive = blk < n_sb
    sel = jnp.zeros((TQ, LANES), F32)
    for _ in range(min(N_SEL, n_sb)):
        best = jnp.max(jnp.where(alive, val, -jnp.inf), axis=1, keepdims=True)
        first = jnp.min(jnp.where(alive & (val == best), blk_f, float(LANES)), axis=1, keepdims=True)
        pick = blk_f == first
        sel = jnp.where(pick, 1.0, sel)
        alive = alive & jnp.logical_not(pick)
    sel_c = sel.astype(CDT)

    def near_bias(near):
        return tokb_ref[:, near * KC:(near + 1) * KC]

    def sel_attend(c, carry, near):
        logits = _dot_nt(q_stack, sk_ref[0, chunk(c), :]) * scale
        e_blk = lax.broadcasted_iota(I32, (LANES, KC), 0)
        e_key = lax.broadcasted_iota(I32, (LANES, KC), 1)
        expand = jnp.where(e_blk == c * (KC // SEL_BLOCK) + (e_key >> (SEL_BLOCK.bit_length() - 1)), 1.0, 0.0).astype(CDT)
        mask = _dot(sel_c, expand) > 0.5
        if near is not None:
            logits = logits + near_bias(near)
            mask = mask & (c * KC + lane <= t_pos)
        return _softmax_step(carry, logits, _tile4(mask), sv_ref[0, chunk(c), :])

    cs = _softmax_init(B_HEADS * TQ)
    cs = lax.fori_loop(0, jnp.maximum(m - 1, 0), lambda c, cr: sel_attend(c, cr, None), cs)
    cs = lax.cond(m >= 1, lambda cr: sel_attend(m - 1, cr, 0), lambda cr: cr, cs)
    cs = sel_attend(m, cs, 1)
    osel = _softmax_out(cs)

    def win_attend(c, carry, near):
        logits = _dot_nt(q_stack, wk_ref[0, chunk(c), :]) * scale
        dist = t_pos - (c * KC + lane)
        mask = (dist >= 0) & (dist < WINDOW)
        if near is not None:
            logits = logits + near_bias(near)
        return _softmax_step(carry, logits, _tile4(mask), wv_ref[0, chunk(c), :])

    cw = _softmax_init(B_HEADS * TQ)
    for back in range(WINDOW // KC, 1, -1):
        cw = lax.cond(m >= back, lambda cr, back=back: win_attend(m - back, cr, None), lambda cr: cr, cw)
    cw = lax.cond(m >= 1, lambda cr: win_attend(m - 1, cr, 0), lambda cr: cr, cw)
    cw = win_attend(m, cw, 1)
    ow = _softmax_out(cw)

    gate = jax.nn.sigmoid(small_ref[0][:, SM_G:SM_G + 3 * B_HEADS])
    outs = []
    for h in range(B_HEADS):
        rows = slice(h * TQ, (h + 1) * TQ)
        outs.append(gate[:, h:h + 1] * oc[rows]
                    + gate[:, B_HEADS + h:B_HEADS + h + 1] * osel[rows]
                    + gate[:, 2 * B_HEADS + h:2 * B_HEADS + h + 1] * ow[rows])
    o_ref[0] = jnp.concatenate(outs, axis=1).astype(o_ref.dtype)


def _nsa(bq, small, ck, cv, sk, sv, wk, wv, tokb, cmpb):
    b, s, _ = bq.shape
    n_c = s // CMP_STRIDE
    qspec = lambda w: pl.BlockSpec((1, TQ, w), lambda i, j: (i, j, 0))
    kspec = lambda n: pl.BlockSpec((1, n, HEAD_DIM), lambda i, j: (i, 0, 0))
    return pl.pallas_call(
        functools.partial(_nsa_kernel, seq=s),
        grid=(b, s // TQ),
        in_specs=[qspec(EV_BQ), qspec(LANES), kspec(n_c), kspec(n_c), kspec(s), kspec(s), kspec(s), kspec(s),
                  pl.BlockSpec((B_HEADS * TQ, 2 * KC), lambda i, j: (1, 0)),
                  pl.BlockSpec((B_HEADS * TQ, LANES), lambda i, j: (1, 0))],
        out_specs=qspec(EV_BQ),
        out_shape=jax.ShapeDtypeStruct((b, s, EV_BQ), CDT),
        compiler_params=_cparams(("parallel", "arbitrary")),
        name="nsa_attention",
    )(bq, small, ck, cv, sk, sv, wk, wv, tokb, cmpb)


def _proj_resid_kernel(*refs, n_in):
    a_refs, w_refs, r_ref, o_ref = refs[:n_in], refs[n_in:2 * n_in], refs[2 * n_in], refs[2 * n_in + 1]
    acc = r_ref[...]
    for a_ref, w_ref in zip(a_refs, w_refs):
        acc = acc + _dot(a_ref[...], w_ref[...])
    o_ref[...] = acc


def _proj_resid(acts, weights, resid, tm=512):
    m, d = resid.shape
    row = lambda width: pl.BlockSpec((tm, width), lambda i: (i, 0))
    full = lambda a: pl.BlockSpec(a.shape, lambda i: (0, 0))
    weights = [w.astype(CDT) for w in weights]
    return pl.pallas_call(
        functools.partial(_proj_resid_kernel, n_in=len(acts)),
        grid=(m // tm,),
        in_specs=[row(a.shape[1]) for a in acts] + [full(w) for w in weights] + [row(d)],
        out_specs=row(d),
        out_shape=jax.ShapeDtypeStruct((m, d), F32),
        compiler_params=_cparams(("parallel",)),
        name="proj_residual",
    )(*acts, *weights, resid)


def _rms_matmul_kernel(x_ref, g_ref, w_ref, o_ref):
    h = _rms(x_ref[...], g_ref[...]).astype(CDT)
    o_ref[...] = _dot(h, w_ref[...]).astype(o_ref.dtype)


def _rms_matmul(x2d, g, w, tm=512, tn=1024):
    m, d = x2d.shape
    n = w.shape[1]
    return pl.pallas_call(
        _rms_matmul_kernel,
        grid=(m // tm, n // tn),
        in_specs=[pl.BlockSpec((tm, d), lambda i, j: (i, 0)),
                  pl.BlockSpec((1, d), lambda i, j: (0, 0)),
                  pl.BlockSpec((d, tn), lambda i, j: (0, j))],
        out_specs=pl.BlockSpec((tm, tn), lambda i, j: (i, j)),
        out_shape=jax.ShapeDtypeStruct((m, n), CDT),
        compiler_params=_cparams(("parallel", "parallel")),
        name="rms_matmul",
    )(x2d, g.reshape(1, d), w.astype(CDT))


def _stick_kernel(q_ref, k_ref, v_ref, o_ref):
    m = pl.program_id(2)
    scale = HEAD_DIM ** -0.5
    q = q_ref[0]
    per = SB_T // SB_G
    t_pos = m * SB_T + lax.broadcasted_iota(I32, (SB_T, 1), 0)
    lane = lax.broadcasted_iota(I32, (1, SB_T), 1)
    n_terms = 1 if CDT == F32 else 2
    uj = lax.broadcasted_iota(I32, (SB_G, 2 * SB_G), 0)
    us = lax.broadcasted_iota(I32, (SB_G, 2 * SB_G), 1)
    suffix = jnp.where((us >= SB_G) | (uj >= us), 1.0, 0.0).astype(CDT)

    def chunk(c, carry, masked):
        run, acc = carry
        rows = pl.ds(pl.multiple_of(c * SB_T, SB_T), SB_T)
        z = _dot_nt(q, k_ref[0, rows, :]) * scale
        log_1m = -(jnp.maximum(z, 0.0) + jnp.log(1.0 + jnp.exp(-jnp.abs(z))))
        if masked:
            strict = c * SB_T + lane < t_pos
            log_1m = jnp.where(strict, log_1m, 0.0)
        stacked = jnp.concatenate([log_1m[:, g * SB_G:(g + 1) * SB_G] for g in range(per)], axis=0)
        sums = _dot_split(stacked, suffix, n_terms)
        later = [None] * per
        for g in range(per - 1, -1, -1):
            later[g] = run + sums[g * SB_T:(g + 1) * SB_T, :SB_G]
            run = run + sums[g * SB_T:(g + 1) * SB_T, SB_G:]
        a = jnp.exp(z + jnp.concatenate(later, axis=1))
        if masked:
            a = jnp.where(strict, a, 0.0)
        acc = acc + _dot(a.astype(CDT), v_ref[0, rows, :])
        return run, acc

    carry = (jnp.zeros((SB_T, SB_G), F32), jnp.zeros((SB_T, HEAD_DIM), F32))
    carry = chunk(m, carry, True)
    carry = lax.fori_loop(0, m, lambda i, cr: chunk(m - 1 - i, cr, False), carry)
    o_ref[0] = carry[1].astype(o_ref.dtype)


def _stick(qkv, b, s):
    h = C_HEADS
    return pl.pallas_call(
        _stick_kernel,
        grid=(b, h, s // SB_T),
        in_specs=[pl.BlockSpec((1, SB_T, HEAD_DIM), lambda i, j, k: (i, k, j)),
                  pl.BlockSpec((1, s, HEAD_DIM), lambda i, j, k: (i, 0, h + j)),
                  pl.BlockSpec((1, s, HEAD_DIM), lambda i, j, k: (i, 0, 2 * h + j))],
        out_specs=pl.BlockSpec((1, SB_T, HEAD_DIM), lambda i, j, k: (i, k, j)),
        out_shape=jax.ShapeDtypeStruct((b, s, h * HEAD_DIM), CDT),
        compiler_params=_cparams(("parallel", "parallel", "arbitrary")),
        name="stick_breaking",
    )(qkv, qkv, qkv)


def _ffn_kernel(*refs, final):
    it = iter(refs)
    x_ref, g_ref, wg_ref, wu_ref, wd_ref = next(it), next(it), next(it), next(it), next(it)
    fg_ref = next(it) if final else None
    o_ref, h_scr, acc_scr = next(it), next(it), next(it)
    f = pl.program_id(1)

    @pl.when(f == 0)
    def _():
        h_scr[...] = _rms(x_ref[...], g_ref[...]).astype(CDT)
        acc_scr[...] = jnp.zeros(acc_scr.shape, F32)

    h = h_scr[...]
    act = jax.nn.silu(_dot(h, wg_ref[...])) * _dot(h, wu_ref[...])
    acc_scr[...] += _dot(act.astype(CDT), wd_ref[...])

    @pl.when(f == pl.num_programs(1) - 1)
    def _():
        out = x_ref[...] + acc_scr[...]
        if final:
            out = _rms(out, fg_ref[...])
        o_ref[...] = out


def _ffn(x2d, g, w_gate, w_up, w_down, final_g=None, tm=512, tf=None):
    m, d = x2d.shape
    ff = w_gate.shape[1]
    final = final_g is not None
    row = pl.BlockSpec((tm, d), lambda i, f: (i, 0))
    vec = pl.BlockSpec((1, d), lambda i, f: (0, 0))
    ins = [x2d, g.reshape(1, d), w_gate.astype(CDT), w_up.astype(CDT), w_down.astype(CDT)]
    specs = [row, vec, pl.BlockSpec((d, tf), lambda i, f: (0, f)), pl.BlockSpec((d, tf), lambda i, f: (0, f)),
             pl.BlockSpec((tf, d), lambda i, f: (f, 0))]
    if final:
        ins.append(final_g.reshape(1, d))
        specs.append(vec)
    return pl.pallas_call(
        functools.partial(_ffn_kernel, final=final),
        grid=(m // tm, ff // tf),
        in_specs=specs,
        out_specs=row,
        out_shape=jax.ShapeDtypeStruct((m, d), F32),
        scratch_shapes=[pltpu.VMEM((tm, d), CDT), pltpu.VMEM((tm, d), F32)],
        compiler_params=_cparams(("parallel", "arbitrary")),
        name="swiglu",
    )(*ins)


MOE_TM = 512
MOE_TT = 512
RT_E1, RT_E2, RT_W1, RT_W2, RT_R1, RT_R2 = range(6)


def _router_kernel(x_ref, g_ref, wr_ref, h_ref, meta_ref, cnt_ref, base_scr):
    i = pl.program_id(0)

    @pl.when(i == 0)
    def _():
        base_scr[...] = jnp.zeros(base_scr.shape, F32)

    hn = _rms(x_ref[...], g_ref[...])
    h_ref[...] = hn
    logits = jnp.dot(hn, wr_ref[...], preferred_element_type=F32, precision=lax.Precision.HIGHEST)
    lane = lax.broadcasted_iota(I32, logits.shape, 1)
    logits = jnp.where(lane < N_EXPERTS, logits, -jnp.inf)
    v1 = jnp.max(logits, axis=1, keepdims=True)
    i1 = jnp.min(jnp.where(logits == v1, lane, LANES), axis=1, keepdims=True)
    rest = jnp.where(lane == i1, -jnp.inf, logits)
    v2 = jnp.max(rest, axis=1, keepdims=True)
    i2 = jnp.min(jnp.where(rest == v2, lane, LANES), axis=1, keepdims=True)
    e2 = jnp.exp(v2 - v1)
    den = 1.0 + e2
    oh1, oh2 = lane == i1, lane == i2
    hits = jnp.where(oh1 | oh2, 1.0, 0.0)
    tm = hits.shape[0]
    earlier = jnp.where(lax.broadcasted_iota(I32, (tm, tm), 1) < lax.broadcasted_iota(I32, (tm, tm), 0), 1.0, 0.0)
    before = _dot(earlier.astype(CDT), hits.astype(CDT)) + base_scr[...]
    r1 = jnp.sum(jnp.where(oh1, before, 0.0), axis=1, keepdims=True)
    r2 = jnp.sum(jnp.where(oh2, before, 0.0), axis=1, keepdims=True)
    base_scr[...] += jnp.sum(hits, axis=0, keepdims=True)
    cnt_ref[...] = base_scr[...]
    meta = jnp.zeros(logits.shape, F32)
    for k, v in ((RT_E1, i1.astype(F32)), (RT_E2, i2.astype(F32)), (RT_W1, 1.0 / den), (RT_W2, e2 / den),
                 (RT_R1, r1), (RT_R2, r2)):
        meta = jnp.where(lane == k, v, meta)
    meta_ref[...] = meta


def _router(x2d, g, w_router, tm=512):
    m, d = x2d.shape
    wr = jnp.concatenate([w_router, jnp.zeros((d, LANES - w_router.shape[1]), w_router.dtype)], axis=1)
    return pl.pallas_call(
        _router_kernel,
        grid=(m // tm,),
        in_specs=[pl.BlockSpec((tm, d), lambda i: (i, 0)), pl.BlockSpec((1, d), lambda i: (0, 0)),
                  pl.BlockSpec((d, LANES), lambda i: (0, 0))],
        out_specs=(pl.BlockSpec((tm, d), lambda i: (i, 0)), pl.BlockSpec((tm, LANES), lambda i: (i, 0)),
                   pl.BlockSpec((1, LANES), lambda i: (0, 0))),
        out_shape=(jax.ShapeDtypeStruct((m, d), F32), jax.ShapeDtypeStruct((m, LANES), F32),
                   jax.ShapeDtypeStruct((1, LANES), F32)),
        scratch_shapes=[pltpu.VMEM((1, LANES), F32)],
        compiler_params=_cparams(("arbitrary",)),
        name="moe_router",
    )(x2d, g.reshape(1, d), wr)


def _row_copy(src_hbm, src_row, dst_hbm, dst_row, sem):
    return pltpu.make_async_copy(src_hbm.at[pl.ds(src_row, 1), :], dst_hbm.at[pl.ds(dst_row, 1), :], sem)


def _dispatch_kernel(off_ref, e1_ref, r1_ref, e2_ref, r2_ref, h_hbm, xs_in_hbm, xs_hbm, sem):
    del xs_in_hbm
    t0 = pl.program_id(0) * MOE_TT

    def start(r, carry):
        _row_copy(h_hbm, t0 + r, xs_hbm, off_ref[e1_ref[0, 0, r]] + r1_ref[0, 0, r], sem).start()
        _row_copy(h_hbm, t0 + r, xs_hbm, off_ref[e2_ref[0, 0, r]] + r2_ref[0, 0, r], sem).start()
        return carry

    def wait(r, carry):
        _row_copy(h_hbm, 0, xs_hbm, 0, sem).wait()
        _row_copy(h_hbm, 0, xs_hbm, 0, sem).wait()
        return carry

    lax.fori_loop(0, MOE_TT, start, 0)
    lax.fori_loop(0, MOE_TT, wait, 0)


def _dispatch(h, off, e1, r1, e2, r2, n_rows):
    m, d = h.shape
    tok = pl.BlockSpec((1, 1, MOE_TT), lambda i, off: (i, 0, 0), memory_space=pltpu.SMEM)
    anyspec = pl.BlockSpec(memory_space=pl.ANY)
    return pl.pallas_call(
        _dispatch_kernel,
        grid_spec=pltpu.PrefetchScalarGridSpec(
            num_scalar_prefetch=1, grid=(m // MOE_TT,),
            in_specs=[tok, tok, tok, tok, anyspec, anyspec],
            out_specs=anyspec,
            scratch_shapes=[pltpu.SemaphoreType.DMA(())]),
        out_shape=jax.ShapeDtypeStruct((n_rows, d), F32),
        input_output_aliases={6: 0},
        compiler_params=_cparams(("arbitrary",)),
        name="moe_dispatch",
    )(off, e1, r1, e2, r2, h, jnp.zeros((n_rows, d), F32))


def _expert_ffn_kernel(te_ref, nu_ref, xs_ref, wg_ref, wu_ref, wd_ref, y_ref, h_scr, acc_scr):
    del te_ref
    i, f = pl.program_id(0), pl.program_id(1)
    used = i < nu_ref[0]

    @pl.when(used & (f == 0))
    def _():
        h_scr[...] = xs_ref[...].astype(CDT)
        acc_scr[...] = jnp.zeros(acc_scr.shape, F32)

    @pl.when(used)
    def _():
        h = h_scr[...]
        act = jax.nn.silu(_dot(h, wg_ref[0])) * _dot(h, wu_ref[0])
        acc_scr[...] += _dot(act.astype(CDT), wd_ref[0])

    @pl.when(f == pl.num_programs(1) - 1)
    def _():
        y_ref[...] = jnp.where(used, acc_scr[...], 0.0)


def _expert_ffn(xs, tile_expert, n_used, w_gate, w_up, w_down, tf):
    n_rows, d = xs.shape
    ff = w_gate.shape[2]
    row = pl.BlockSpec((MOE_TM, d), lambda i, f, te, nu: (i, 0))
    return pl.pallas_call(
        _expert_ffn_kernel,
        grid_spec=pltpu.PrefetchScalarGridSpec(
            num_scalar_prefetch=2, grid=(n_rows // MOE_TM, ff // tf),
            in_specs=[row,
                      pl.BlockSpec((1, d, tf), lambda i, f, te, nu: (te[i], 0, f)),
                      pl.BlockSpec((1, d, tf), lambda i, f, te, nu: (te[i], 0, f)),
                      pl.BlockSpec((1, tf, d), lambda i, f, te, nu: (te[i], f, 0))],
            out_specs=row,
            scratch_shapes=[pltpu.VMEM((MOE_TM, d), CDT), pltpu.VMEM((MOE_TM, d), F32)]),
        out_shape=jax.ShapeDtypeStruct((n_rows, d), F32),
        compiler_params=_cparams(("arbitrary", "arbitrary")),
        name="moe_expert_swiglu",
    )(tile_expert, n_used, xs, w_gate.astype(CDT), w_up.astype(CDT), w_down.astype(CDT))


def _combine_kernel(off_ref, e1_ref, r1_ref, e2_ref, r2_ref, x_ref, meta_ref, fg_ref, y_hbm, o_ref,
                    buf1, buf2, sem, *, final):
    def start(r, carry):
        pltpu.make_async_copy(y_hbm.at[pl.ds(off_ref[e1_ref[0, 0, r]] + r1_ref[0, 0, r], 1), :],
                              buf1.at[pl.ds(r, 1), :], sem).start()
        pltpu.make_async_copy(y_hbm.at[pl.ds(off_ref[e2_ref[0, 0, r]] + r2_ref[0, 0, r], 1), :],
                              buf2.at[pl.ds(r, 1), :], sem).start()
        return carry

    def wait(r, carry):
        pltpu.make_async_copy(y_hbm.at[pl.ds(0, 1), :], buf1.at[pl.ds(0, 1), :], sem).wait()
        pltpu.make_async_copy(y_hbm.at[pl.ds(0, 1), :], buf2.at[pl.ds(0, 1), :], sem).wait()
        return carry

    lax.fori_loop(0, MOE_TT, start, 0)
    lax.fori_loop(0, MOE_TT, wait, 0)
    meta = meta_ref[...]
    out = x_ref[...] + (meta[:, RT_W1:RT_W1 + 1] * buf1[...] + meta[:, RT_W2:RT_W2 + 1] * buf2[...])
    if final:
        out = _rms(out, fg_ref[...])
    o_ref[...] = out


def _combine(x2d, meta, y, off, e1, r1, e2, r2, final_g):
    m, d = x2d.shape
    final = final_g is not None
    fg = (final_g if final else jnp.ones((d,), F32)).reshape(1, d)
    tok = pl.BlockSpec((1, 1, MOE_TT), lambda i, off: (i, 0, 0), memory_space=pltpu.SMEM)
    return pl.pallas_call(
        functools.partial(_combine_kernel, final=final),
        grid_spec=pltpu.PrefetchScalarGridSpec(
            num_scalar_prefetch=1, grid=(m // MOE_TT,),
            in_specs=[tok, tok, tok, tok,
                      pl.BlockSpec((MOE_TT, d), lambda i, off: (i, 0)),
                      pl.BlockSpec((MOE_TT, LANES), lambda i, off: (i, 0)),
                      pl.BlockSpec((1, d), lambda i, off: (0, 0)),
                      pl.BlockSpec(memory_space=pl.ANY)],
            out_specs=pl.BlockSpec((MOE_TT, d), lambda i, off: (i, 0)),
            scratch_shapes=[pltpu.VMEM((MOE_TT, d), F32), pltpu.VMEM((MOE_TT, d), F32),
                            pltpu.SemaphoreType.DMA(())]),
        out_shape=jax.ShapeDtypeStruct((m, d), F32),
        compiler_params=_cparams(("arbitrary",)),
        name="moe_combine",
    )(off, e1, r1, e2, r2, x2d, meta, fg, y)


def _moe(x2d, g, w_router, w_gate, w_up, w_down, final_g, tf):
    m, d = x2d.shape
    n_e = w_gate.shape[0]
    h, meta, cnt = _router(x2d, g, w_router)
    n_tiles = (2 * m) // MOE_TM + n_e
    padded = ((cnt[0, :n_e].astype(I32) + MOE_TM - 1) // MOE_TM) * MOE_TM
    ends = jnp.cumsum(padded)
    off = ends - padded
    tile_expert = jnp.minimum(jnp.searchsorted(ends, jnp.arange(n_tiles, dtype=I32) * MOE_TM, side="right"),
                              n_e - 1).astype(I32)
    n_used = (ends[n_e - 1:] // MOE_TM).astype(I32)
    ids = [meta[:, k].astype(I32).reshape(m // MOE_TT, 1, MOE_TT) for k in (RT_E1, RT_R1, RT_E2, RT_R2)]
    xs = _dispatch(h, off, *ids, n_tiles * MOE_TM)
    y = _expert_ffn(xs, tile_expert, n_used, w_gate, w_up, w_down, tf)
    return _combine(x2d, meta, y, off, *ids, final_g)


def _even_mixer(x2d, b, s, g, w_in, kv_norm, w_uk, w_uv, cmp_pos, cmp_w1, cmp_w2, w_out, tokb, tokbt, cmpb):
    (aqt, iqt, smallt, ak, avt, ik, bq, ckr, cvr, sk, sv, wk, wv, small) = _even_inproj(
        x2d, g, w_in, kv_norm, w_uk, w_uv)
    r3 = lambda a: a.reshape(b, s, a.shape[1])
    oa = _dsa(aqt, iqt, smallt, r3(ik), r3(ak), avt, tokbt)
    cmp = _compress(jnp.stack([ckr, cvr]).reshape(2, b, s, HEAD_DIM), cmp_pos, cmp_w1, cmp_w2)
    ob = _nsa(r3(bq), r3(small), cmp[0], cmp[1], r3(sk), r3(sv), r3(wk), r3(wv), tokb, cmpb)
    return _proj_resid([oa.reshape(b * s, EV_AQ), ob.reshape(b * s, EV_BQ)],
                       [w_out[:EV_AQ], w_out[EV_AQ:]], x2d)


def kernel(x, rel_bias, ev_norm_mix, ev_w_in, ev_kv_norm, ev_w_uk, ev_w_uv, ev_cmp_pos, ev_cmp_w1, ev_cmp_w2, ev_w_out, ev_norm_ffn, ev_w_gate, ev_w_up, ev_w_down, od_norm_mix, od_w_qkv, od_w_out, od_norm_ffn, od_w_router, od_w_gate, od_w_up, od_w_down, final_norm):
    b, s, d = x.shape
    depth = ev_norm_mix.shape[0] + od_norm_mix.shape[0]
    tokb, tokbt, cmpb = _bias_tiles(rel_bias)
    x2d = x.reshape(b * s, d)
    for layer in range(depth):
        i = layer // 2
        last = layer == depth - 1
        if layer % 2 == 0:
            x2d = _even_mixer(x2d, b, s, ev_norm_mix[i], ev_w_in[i], ev_kv_norm[i], ev_w_uk[i], ev_w_uv[i],
                              ev_cmp_pos[i], ev_cmp_w1[i], ev_cmp_w2[i], ev_w_out[i], tokb, tokbt, cmpb)
            x2d = _ffn(x2d, ev_norm_ffn[i], ev_w_gate[i], ev_w_up[i], ev_w_down[i],
                       final_g=final_norm if last else None, tf=ev_w_gate.shape[2] // 2)
        else:
            qkv = _rms_matmul(x2d, od_norm_mix[i], od_w_qkv[i])
            o = _stick(qkv.reshape(b, s, qkv.shape[1]), b, s)
            x2d = _proj_resid([o.reshape(b * s, o.shape[2])], [od_w_out[i]], x2d)
            x2d = _moe(x2d, od_norm_ffn[i], od_w_router[i], od_w_gate[i], od_w_up[i], od_w_down[i],
                       final_norm if last else None, tf=od_w_gate.shape[3] // 4)
    return x2d.reshape(b, s, d)
```

```python
import functools
import math

import numpy as np
import jax
import jax.numpy as jnp
from jax import lax
from jax.experimental import pallas as pl
from jax.experimental.pallas import tpu as pltpu

HEAD_DIM = 128
A_HEADS = 4
A_KV_RANK = 256
IDX_HEADS = 4
IDX_DIM = 64
DSA_TOPK = 256
B_HEADS = 4
CMP_LEN = 32
CMP_STRIDE = 16
SEL_BLOCK = 64
N_SEL = 8
WINDOW = 512
C_HEADS = 8
NUM_BUCKETS = 32
MAX_DISTANCE = 128
N_EXPERTS = 8
RMS_EPS = 1e-6
NEG_BIG = -1e30

LANES = 128
CDT = jnp.bfloat16
F32 = jnp.float32
I32 = jnp.int32
INT_MIN = -2 ** 31
KEY_NEG_INF = -2 ** 31 + 0x7FFFFF
VMEM_LIMIT = 56 * 1024 * 1024

TQ = 256
KC = 256
SB_G = 256
SB_T = 512
SB_HEADS = 2


def _cparams(sem):
    return pltpu.CompilerParams(dimension_semantics=sem, vmem_limit_bytes=VMEM_LIMIT)


def _dot(a, b):
    return jnp.dot(a, b, preferred_element_type=F32)


def _dot_nt(a, b):
    return lax.dot_general(a, b, (((1,), (1,)), ((), ())), preferred_element_type=F32)


def _rms(x, g):
    return x * lax.rsqrt(jnp.mean(x * x, axis=-1, keepdims=True) + RMS_EPS) * g


def _split_terms(x, n):
    if CDT == F32:
        return [x]
    out, r = [], x
    for _ in range(n):
        h = r.astype(CDT)
        out.append(h)
        r = r - h.astype(F32)
    return out


def _dot_split(x, w01, n):
    terms = _split_terms(x, n)
    return _dot(jnp.concatenate(terms, axis=1), jnp.concatenate([w01] * len(terms), axis=0))


def _sortable(v):
    bits = lax.bitcast_convert_type(v, I32)
    key = bits ^ ((bits >> 31) & 0x7FFFFFFF)
    return jnp.where(v == 0.0, 0, key)


def _stack_heads(x, n, w):
    return jnp.concatenate([x[:, h * w:(h + 1) * w] for h in range(n)], axis=0)


def _unstack_heads(x, n, t):
    return jnp.concatenate([x[h * t:(h + 1) * t] for h in range(n)], axis=1)


def _bucket_np(dist):
    n = np.maximum(dist, 0)
    max_exact = NUM_BUCKETS // 2
    large = max_exact + (np.log(np.maximum(n, 1).astype(np.float32) / np.float32(max_exact))
                         / np.float32(math.log(MAX_DISTANCE / max_exact))
                         * np.float32(NUM_BUCKETS - max_exact)).astype(np.int32)
    large = np.minimum(large, NUM_BUCKETS - 1)
    return np.where(n < max_exact, n, large).astype(np.int32)


def _bias_tile_kernel(bias_ref, tokbk_ref, tokbkt_ref, cmpbk_ref, tok_ref, tokt_ref, cmp_ref, *, n_heads):
    tb = tokbk_ref[...]
    tbt = tokbkt_ref[...]
    cb = cmpbk_ref[...]
    for h in range(n_heads):
        far = bias_ref[NUM_BUCKETS - 1, h]
        tacc = jnp.zeros(tb.shape, F32)
        ttacc = jnp.zeros(tbt.shape, F32)
        cacc = jnp.zeros(cb.shape, F32)
        for k in range(NUM_BUCKETS - 1):
            v = bias_ref[k, h] - far
            tacc = jnp.where(tb == k, v, tacc)
            ttacc = jnp.where(tbt == k, v, ttacc)
            cacc = jnp.where(cb == k, v, cacc)
        tok_ref[h * TQ:(h + 1) * TQ, :] = tacc
        tokt_ref[:, h * TQ:(h + 1) * TQ] = ttacc
        cmp_ref[h * TQ:(h + 1) * TQ, :] = cacc


def _bias_tiles(rel_bias):
    n_heads = rel_bias.shape[1]
    i = np.arange(TQ)[:, None]
    j = np.arange(2 * KC)[None, :]
    dist = i + KC - j
    tokbk = np.where(dist >= 0, _bucket_np(dist), NUM_BUCKETS - 1).astype(np.int32)
    r = np.arange(LANES)[None, :]
    cdist = i + (TQ - CMP_LEN + 1) - CMP_STRIDE * r
    cmpbk = np.where((cdist >= 0) & (r < 2 * TQ // CMP_STRIDE), _bucket_np(cdist), NUM_BUCKETS - 1).astype(np.int32)
    return pl.pallas_call(
        functools.partial(_bias_tile_kernel, n_heads=n_heads),
        out_shape=(jax.ShapeDtypeStruct((n_heads * TQ, 2 * KC), F32),
                   jax.ShapeDtypeStruct((2 * KC, n_heads * TQ), F32),
                   jax.ShapeDtypeStruct((n_heads * TQ, LANES), F32)),
        in_specs=[pl.BlockSpec(memory_space=pltpu.SMEM)] + [pl.BlockSpec(memory_space=pltpu.VMEM)] * 3,
        out_specs=(pl.BlockSpec(memory_space=pltpu.VMEM),) * 3,
        compiler_params=pltpu.CompilerParams(vmem_limit_bytes=VMEM_LIMIT),
        name="bias_tiles",
    )(rel_bias, jnp.asarray(tokbk), jnp.asarray(tokbk.T.copy()), jnp.asarray(cmpbk))


EV_AQ = A_HEADS * HEAD_DIM
EV_IQ = IDX_HEADS * IDX_DIM
EV_BQ = B_HEADS * HEAD_DIM
EV_OFF_LAT = 0
EV_OFF_BQ = EV_OFF_LAT + A_KV_RANK
EV_OFF_KV = EV_OFF_BQ + EV_BQ
EV_OFF_SMALL = EV_OFF_KV + 6 * HEAD_DIM
EV_TOFF_AQ = 0
EV_TOFF_IQ = EV_TOFF_AQ + EV_AQ
EV_TOFF_SMALL = EV_TOFF_IQ + EV_IQ
SM_IK = 0
SM_IW = IDX_DIM
SM_G = IDX_DIM + IDX_HEADS


def _even_inproj_kernel(x_ref, g_ref, w_ref, wt_ref, kvg_ref, wuk_ref, wuvt_ref,
                        aqt_ref, iqt_ref, smallt_ref, ak_ref, avt_ref, ik_ref, bq_ref,
                        ckr_ref, cvr_ref, sk_ref, sv_ref, wk_ref, wv_ref, small_ref):
    h = _rms(x_ref[...], g_ref[...]).astype(CDT)

    def seg(off, width):
        return _dot(h, w_ref[:, off:off + width])

    def seg_t(off, width):
        return _dot_nt(wt_ref[off:off + width, :], h)

    aqt_ref[...] = seg_t(EV_TOFF_AQ, EV_AQ).astype(aqt_ref.dtype)
    iqt_ref[...] = seg_t(EV_TOFF_IQ, EV_IQ).astype(iqt_ref.dtype)
    smallt_ref[...] = seg_t(EV_TOFF_SMALL, LANES)
    lat = seg(EV_OFF_LAT, A_KV_RANK)
    ckv = _rms(lat, kvg_ref[...]).astype(CDT)
    ak_ref[...] = _dot(ckv, wuk_ref[...]).astype(ak_ref.dtype)
    avt_ref[...] = _dot_nt(wuvt_ref[...], ckv).astype(avt_ref.dtype)
    bq_ref[...] = seg(EV_OFF_BQ, EV_BQ).astype(bq_ref.dtype)
    kv = seg(EV_OFF_KV, 6 * HEAD_DIM)
    for j, ref in enumerate((ckr_ref, cvr_ref, sk_ref, sv_ref, wk_ref, wv_ref)):
        ref[...] = kv[:, j * HEAD_DIM:(j + 1) * HEAD_DIM].astype(ref.dtype)
    small = seg(EV_OFF_SMALL, LANES)
    small_ref[...] = small
    ik_ref[...] = small[:, SM_IK:SM_IK + IDX_DIM].astype(ik_ref.dtype)


def _even_inproj(x2d, g, w_in, kv_norm, w_uk, w_uv, tm=512):
    m, d = x2d.shape
    offs = np.cumsum((EV_AQ, A_KV_RANK, EV_IQ, IDX_DIM, IDX_HEADS, EV_BQ, 6 * HEAD_DIM, 3 * B_HEADS))
    aq, lat, iq, ik, iw, bq, bkv, bg = jnp.split(w_in, [int(o) for o in offs[:-1]], axis=1)
    pad = jnp.zeros((d, LANES - IDX_DIM - IDX_HEADS - 3 * B_HEADS), w_in.dtype)
    small_w = jnp.concatenate([ik, iw, bg, pad], axis=1)
    w = jnp.concatenate([lat, bq, bkv, small_w], axis=1).astype(CDT)
    wt = jnp.concatenate([aq, iq, small_w], axis=1).T.astype(CDT)
    row = lambda width: pl.BlockSpec((tm, width), lambda i: (i, 0))
    col = lambda height: pl.BlockSpec((height, tm), lambda i: (0, i))
    full = lambda a: pl.BlockSpec(a.shape, lambda i: (0,) * a.ndim)
    g2, kvg2 = g.reshape(1, d), kv_norm.reshape(1, A_KV_RANK)
    wuk, wuvt = w_uk.astype(CDT), w_uv.T.astype(CDT)
    heights = (EV_AQ, EV_IQ, LANES)
    widths = (HEAD_DIM, None, IDX_DIM, EV_BQ) + (HEAD_DIM,) * 6 + (LANES,)
    dtypes = (CDT,) * 4 + (F32, F32) + (CDT,) * 4 + (F32,)
    out_specs = [col(ht) for ht in heights] + [col(HEAD_DIM) if wd is None else row(wd) for wd in widths]
    out_shape = ([jax.ShapeDtypeStruct((ht, m), dt) for ht, dt in zip(heights, (CDT, CDT, F32))]
                 + [jax.ShapeDtypeStruct((HEAD_DIM, m) if wd is None else (m, wd), dt)
                    for wd, dt in zip(widths, dtypes)])
    return pl.pallas_call(
        _even_inproj_kernel,
        grid=(m // tm,),
        in_specs=[row(d), full(g2), full(w), full(wt), full(kvg2), full(wuk), full(wuvt)],
        out_specs=tuple(out_specs),
        out_shape=tuple(out_shape),
        compiler_params=_cparams(("parallel",)),
        name="even_inproj",
    )(x2d, g2, w, wt, kvg2, wuk, wuvt)


def _compress_kernel(kv_ref, pos_ref, w1_ref, w2_ref, o_ref):
    kv = kv_ref[0, 0]
    half = kv.shape[1]
    first = _dot((kv + pos_ref[0, :, :half]).astype(CDT), w1_ref[0, :half, :])
    second = _dot((kv + pos_ref[0, :, half:]).astype(CDT), w1_ref[0, half:, :])
    hid = first + pltpu.roll(second, second.shape[0] - 1, 0)
    hid = jax.nn.gelu(hid)
    o_ref[0, 0] = _dot(hid.astype(CDT), w2_ref[0]).astype(o_ref.dtype)


def _compress(raw, cmp_pos, cmp_w1, cmp_w2):
    _, b, s, hd = raw.shape
    g = s // CMP_STRIDE
    kv = raw.reshape(2, b, g, CMP_STRIDE * hd)
    pos = cmp_pos.reshape(2, 1, CMP_LEN * hd)
    return pl.pallas_call(
        _compress_kernel,
        grid=(2, b),
        in_specs=[pl.BlockSpec((1, 1, g, CMP_STRIDE * hd), lambda j, i: (j, i, 0, 0)),
                  pl.BlockSpec((1, 1, CMP_LEN * hd), lambda j, i: (j, 0, 0)),
                  pl.BlockSpec((1, CMP_LEN * hd, hd), lambda j, i: (j, 0, 0)),
                  pl.BlockSpec((1, hd, hd), lambda j, i: (j, 0, 0))],
        out_specs=pl.BlockSpec((1, 1, g, hd), lambda j, i: (j, i, 0, 0)),
        out_shape=jax.ShapeDtypeStruct((2, b, g, hd), CDT),
        compiler_params=_cparams(("parallel", "parallel")),
        name="nsa_compress",
    )(kv, pos, cmp_w1.astype(CDT), cmp_w2.astype(CDT))


def _softmax_step(carry, logits, mask4, v):
    m_i, l_i, acc = carry
    s = jnp.where(mask4, logits, NEG_BIG)
    m_new = jnp.maximum(m_i, jnp.max(s, axis=1, keepdims=True))
    alpha = jnp.exp(m_i - m_new)
    p = jnp.where(mask4, jnp.exp(s - m_new), 0.0)
    l_new = alpha * l_i + jnp.sum(p, axis=1, keepdims=True)
    acc = alpha * acc + _dot(p.astype(CDT), v)
    return m_new, l_new, acc


def _softmax_init(rows):
    return (jnp.full((rows, 1), NEG_BIG, F32), jnp.zeros((rows, 1), F32), jnp.zeros((rows, HEAD_DIM), F32))


def _softmax_out(carry):
    _, l_i, acc = carry
    return jnp.where(l_i > 0.0, acc / jnp.where(l_i > 0.0, l_i, 1.0), 0.0)


def _tile4(mask):
    return jnp.concatenate([mask] * 4, axis=0)


def _dsa_kernel(aqt_ref, iqt_ref, smallt_ref, ik_ref, ak_ref, avt_ref, tokbt_ref, o_ref, key_scr, j_scr,
                *, seq, topk):
    m = pl.program_id(1)
    scale = HEAD_DIM ** -0.5
    lanes4 = A_HEADS * TQ
    q_t = jnp.concatenate([aqt_ref[h * HEAD_DIM:(h + 1) * HEAD_DIM, :] for h in range(A_HEADS)], axis=1)
    iq_t = jnp.concatenate([iqt_ref[h * IDX_DIM:(h + 1) * IDX_DIM, :] for h in range(IDX_HEADS)], axis=1)
    iw = smallt_ref[SM_IW:SM_IW + IDX_HEADS, :] * (IDX_HEADS ** -0.5)
    t_pos = m * TQ + lax.broadcasted_iota(I32, (1, TQ), 1)
    kidx = lax.broadcasted_iota(I32, (KC, 1), 0)

    @pl.when(m == 0)
    def _():
        key_scr[...] = jnp.full(key_scr.shape, INT_MIN, I32)

    def chunk(c):
        return pl.ds(pl.multiple_of(c * KC, KC), KC)

    def score_body(c, carry):
        s = _dot(ik_ref[0, chunk(c), :], iq_t)
        isc = jnp.zeros((KC, TQ), F32)
        for h in range(IDX_HEADS):
            isc = isc + jnp.maximum(s[:, h * TQ:(h + 1) * TQ] * (IDX_DIM ** -0.5), 0.0) * iw[h:h + 1, :]
        isc = jnp.where(c * KC + kidx <= t_pos, isc, -jnp.inf)
        key_scr[chunk(c), :] = _sortable(isc)
        return carry

    lax.fori_loop(0, m + 1, score_body, 0)

    def count(pred):
        def body(c, acc):
            hit = jnp.where(pred(key_scr[chunk(c), :], c * KC + kidx), 1.0, 0.0)
            return acc + jnp.sum(hit.reshape(KC // 8, 8, TQ), axis=0)
        acc = lax.fori_loop(0, m + 1, body, jnp.zeros((8, TQ), F32))
        return jnp.sum(acc, axis=0, keepdims=True)

    kf = float(topk)
    t0 = jnp.where(count(lambda k, _: k >= 0) >= kf, 0, INT_MIN).astype(I32)

    def bit_body(i, t_cur):
        cand = t_cur | lax.shift_left(jnp.int32(1), 30 - i)
        return jnp.where(count(lambda k, _: k >= cand) >= kf, cand, t_cur)

    thr = lax.fori_loop(0, 31, bit_body, t0)
    n_ge = count(lambda k, _: k >= thr)
    n_gt = count(lambda k, _: k > thr)
    need = kf - n_gt
    tied = (n_ge > kf) & (thr > KEY_NEG_INF)
    j_scr[...] = jnp.full(j_scr.shape, seq, I32)

    @pl.when(jnp.max(jnp.where(tied, 1.0, 0.0)) > 0.0)
    def _():
        def jbit(i, j_cur):
            cand = j_cur | lax.shift_left(jnp.int32(1), (seq.bit_length() - 2) - i)
            return jnp.where(count(lambda k, idx: (k == thr) & (idx < cand)) < need, cand, j_cur)
        j_found = lax.fori_loop(0, seq.bit_length() - 1, jbit, jnp.zeros((1, TQ), I32))
        j_scr[...] = jnp.broadcast_to(jnp.where(tied, j_found, seq), j_scr.shape)

    j_last = j_scr[0:1, :]

    def attend(c, carry, bias, causal):
        m_i, l_i, acc = carry
        logits = _dot(ak_ref[0, chunk(c), :], q_t) * scale
        if bias is not None:
            logits = logits + bias
        k = key_scr[chunk(c), :]
        idx = c * KC + kidx
        sel = (k > thr) | ((k == thr) & (idx <= j_last))
        if causal:
            sel = sel & (idx <= t_pos)
        sel4 = jnp.concatenate([sel] * A_HEADS, axis=1)
        s = jnp.where(sel4, logits, NEG_BIG)
        m_new = jnp.maximum(m_i, jnp.max(s, axis=0, keepdims=True))
        alpha = jnp.exp(m_i - m_new)
        p = jnp.where(sel4, jnp.exp(s - m_new), 0.0)
        l_new = alpha * l_i + jnp.sum(p, axis=0, keepdims=True)
        acc = alpha * acc + _dot(avt_ref[:, chunk(c)], p.astype(CDT))
        return m_new, l_new, acc

    carry = (jnp.full((1, lanes4), NEG_BIG, F32), jnp.zeros((1, lanes4), F32), jnp.zeros((HEAD_DIM, lanes4), F32))
    carry = lax.fori_loop(0, jnp.maximum(m - 1, 0), lambda c, cr: attend(c, cr, None, False), carry)
    carry = lax.cond(m >= 1, lambda cr: attend(m - 1, cr, tokbt_ref[:KC, :], False), lambda cr: cr, carry)
    _, l_i, acc = attend(m, carry, tokbt_ref[KC:, :], True)
    o_t = jnp.where(l_i > 0.0, acc / jnp.where(l_i > 0.0, l_i, 1.0), 0.0)
    o_ref[0] = jnp.concatenate([o_t[:, h * TQ:(h + 1) * TQ].T for h in range(A_HEADS)], axis=1).astype(o_ref.dtype)


def _dsa(aqt, iqt, smallt, ik, ak, avt, tokbt):
    b, s, _ = ak.shape
    nq = s // TQ
    topk = min(DSA_TOPK, s // 4)
    qspec = lambda rows: pl.BlockSpec((rows, TQ), lambda i, j: (0, i * nq + j))
    kspec = lambda w: pl.BlockSpec((1, s, w), lambda i, j: (i, 0, 0))
    return pl.pallas_call(
        functools.partial(_dsa_kernel, seq=s, topk=topk),
        grid=(b, nq),
        in_specs=[qspec(EV_AQ), qspec(EV_IQ), qspec(LANES), kspec(IDX_DIM), kspec(HEAD_DIM),
                  pl.BlockSpec((HEAD_DIM, s), lambda i, j: (0, i)),
                  pl.BlockSpec((2 * KC, A_HEADS * TQ), lambda i, j: (0, 0))],
        out_specs=pl.BlockSpec((1, TQ, EV_AQ), lambda i, j: (i, j, 0)),
        out_shape=jax.ShapeDtypeStruct((b, s, EV_AQ), CDT),
        scratch_shapes=[pltpu.VMEM((s, TQ), I32), pltpu.VMEM((8, TQ), I32)],
        compiler_params=_cparams(("parallel", "arbitrary")),
        name="dsa_attention",
    )(aqt, iqt, smallt, ik, ak, avt, tokbt)


def _nsa_kernel(bq_ref, small_ref, ck_ref, cv_ref, sk_ref, sv_ref, wk_ref, wv_ref, tokb_ref, cmpb_ref, o_ref,
                *, seq):
    m = pl.program_id(1)
    scale = HEAD_DIM ** -0.5
    n_c = seq // CMP_STRIDE
    n_sb = seq // SEL_BLOCK
    q_stack = _stack_heads(bq_ref[0], B_HEADS, HEAD_DIM)
    t_pos = m * TQ + lax.broadcasted_iota(I32, (TQ, 1), 0)
    lane = lax.broadcasted_iota(I32, (1, KC), 1)

    def chunk(c):
        return pl.ds(pl.multiple_of(c * KC, KC), KC)

    lc = _dot_nt(q_stack, ck_ref[0]) * scale
    r_i = lax.broadcasted_iota(I32, (LANES, n_c), 0)
    n_i = lax.broadcasted_iota(I32, (LANES, n_c), 1)
    first_n = m * (TQ // CMP_STRIDE) - TQ // CMP_STRIDE
    place = jnp.where(n_i == first_n + r_i, 1.0, 0.0).astype(CDT)
    lc = lc + _dot_split(cmpb_ref[...], place, 2)
    n_row = lax.broadcasted_iota(I32, (1, n_c), 1)
    cvalid = (n_row * CMP_STRIDE + (CMP_LEN - 1) <= t_pos) & (n_row < n_c - 1)
    cv4 = _tile4(cvalid)
    sc = jnp.where(cv4, lc, NEG_BIG)
    pc = jnp.where(cv4, jnp.exp(sc - jnp.max(sc, axis=1, keepdims=True)), 0.0)
    den = jnp.sum(pc, axis=1, keepdims=True)
    pc = jnp.where(den > 0.0, pc / jnp.where(den > 0.0, den, 1.0), 0.0)
    oc = _dot(pc.astype(CDT), cv_ref[0])

    psum = pc[0:TQ]
    for h in range(1, B_HEADS):
        psum = psum + pc[h * TQ:(h + 1) * TQ]
    on = lax.broadcasted_iota(I32, (n_c, LANES), 0) * CMP_STRIDE
    om = lax.broadcasted_iota(I32, (n_c, LANES), 1) * SEL_BLOCK
    overlap = jnp.where((on < om + SEL_BLOCK) & (on + CMP_LEN > om) & (om < seq), 1.0, 0.0).astype(CDT)
    imp = _dot_split(psum, overlap, 3)
    blk = lax.broadcasted_iota(I32, (1, LANES), 1)
    cur = t_pos >> (SEL_BLOCK.bit_length() - 1)
    forced = (blk == 0) | (blk == cur) | (blk == cur - 1)
    val = jnp.where(forced, jnp.inf, jnp.where(blk > cur, -jnp.inf, imp))
    blk_f = blk.astype(F32)
    alive = blk < n_sb
    sel = jnp.zeros((TQ, LANES), F32)
    for _ in range(min(N_SEL, n_sb)):
        best = jnp.max(jnp.where(alive, val, -jnp.inf), axis=1, keepdims=True)
        first = jnp.min(jnp.where(alive & (val == best), blk_f, float(LANES)), axis=1, keepdims=True)
        pick = blk_f == first
        sel = jnp.where(pick, 1.0, sel)
        alive = alive & jnp.logical_not(pick)
    sel_c = sel.astype(CDT)

    def near_bias(near):
        return tokb_ref[:, near * KC:(near + 1) * KC]

    def sel_attend(c, carry, near):
        logits = _dot_nt(q_stack, sk_ref[0, chunk(c), :]) * scale
        e_blk = lax.broadcasted_iota(I32, (LANES, KC), 0)
        e_key = lax.broadcasted_iota(I32, (LANES, KC), 1)
        expand = jnp.where(e_blk == c * (KC // SEL_BLOCK) + (e_key >> (SEL_BLOCK.bit_length() - 1)), 1.0, 0.0).astype(CDT)
        mask = _dot(sel_c, expand) > 0.5
        if near is not None:
            logits = logits + near_bias(near)
            mask = mask & (c * KC + lane <= t_pos)
        return _softmax_step(carry, logits, _tile4(mask), sv_ref[0, chunk(c), :])

    cs = _softmax_init(B_HEADS * TQ)
    cs = lax.fori_loop(0, jnp.maximum(m - 1, 0), lambda c, cr: sel_attend(c, cr, None), cs)
    cs = lax.cond(m >= 1, lambda cr: sel_attend(m - 1, cr, 0), lambda cr: cr, cs)
    cs = sel_attend(m, cs, 1)
    osel = _softmax_out(cs)

    def win_attend(c, carry, near):
        logits = _dot_nt(q_stack, wk_ref[0, chunk(c), :]) * scale
        dist = t_pos - (c * KC + lane)
        mask = (dist >= 0) & (dist < WINDOW)
        if near is not None:
            logits = logits + near_bias(near)
        return _softmax_step(carry, logits, _tile4(mask), wv_ref[0, chunk(c), :])

    cw = _softmax_init(B_HEADS * TQ)
    for back in range(WINDOW // KC, 1, -1):
        cw = lax.cond(m >= back, lambda cr, back=back: win_attend(m - back, cr, None), lambda cr: cr, cw)
    cw = lax.cond(m >= 1, lambda cr: win_attend(m - 1, cr, 0), lambda cr: cr, cw)
    cw = win_attend(m, cw, 1)
    ow = _softmax_out(cw)

    gate = jax.nn.sigmoid(small_ref[0][:, SM_G:SM_G + 3 * B_HEADS])
    outs = []
    for h in range(B_HEADS):
        rows = slice(h * TQ, (h + 1) * TQ)
        outs.append(gate[:, h:h + 1] * oc[rows]
                    + gate[:, B_HEADS + h:B_HEADS + h + 1] * osel[rows]
                    + gate[:, 2 * B_HEADS + h:2 * B_HEADS + h + 1] * ow[rows])
    o_ref[0] = jnp.concatenate(outs, axis=1).astype(o_ref.dtype)


def _nsa(bq, small, ck, cv, sk, sv, wk, wv, tokb, cmpb):
    b, s, _ = bq.shape
    n_c = s // CMP_STRIDE
    qspec = lambda w: pl.BlockSpec((1, TQ, w), lambda i, j: (i, j, 0))
    kspec = lambda n: pl.BlockSpec((1, n, HEAD_DIM), lambda i, j: (i, 0, 0))
    return pl.pallas_call(
        functools.partial(_nsa_kernel, seq=s),
        grid=(b, s // TQ),
        in_specs=[qspec(EV_BQ), qspec(LANES), kspec(n_c), kspec(n_c), kspec(s), kspec(s), kspec(s), kspec(s),
                  pl.BlockSpec((B_HEADS * TQ, 2 * KC), lambda i, j: (1, 0)),
                  pl.BlockSpec((B_HEADS * TQ, LANES), lambda i, j: (1, 0))],
        out_specs=qspec(EV_BQ),
        out_shape=jax.ShapeDtypeStruct((b, s, EV_BQ), CDT),
        compiler_params=_cparams(("parallel", "arbitrary")),
        name="nsa_attention",
    )(bq, small, ck, cv, sk, sv, wk, wv, tokb, cmpb)


def _proj_resid_kernel(*refs, n_in):
    a_refs, w_refs, r_ref, o_ref = refs[:n_in], refs[n_in:2 * n_in], refs[2 * n_in], refs[2 * n_in + 1]
    acc = r_ref[...]
    for a_ref, w_ref in zip(a_refs, w_refs):
        acc = acc + _dot(a_ref[...], w_ref[...])
    o_ref[...] = acc


def _proj_resid(acts, weights, resid, tm=512):
    m, d = resid.shape
    row = lambda width: pl.BlockSpec((tm, width), lambda i: (i, 0))
    full = lambda a: pl.BlockSpec(a.shape, lambda i: (0, 0))
    weights = [w.astype(CDT) for w in weights]
    return pl.pallas_call(
        functools.partial(_proj_resid_kernel, n_in=len(acts)),
        grid=(m // tm,),
        in_specs=[row(a.shape[1]) for a in acts] + [full(w) for w in weights] + [row(d)],
        out_specs=row(d),
        out_shape=jax.ShapeDtypeStruct((m, d), F32),
        compiler_params=_cparams(("parallel",)),
        name="proj_residual",
    )(*acts, *weights, resid)


def _rms_matmul_kernel(x_ref, g_ref, w_ref, o_ref):
    h = _rms(x_ref[...], g_ref[...]).astype(CDT)
    o_ref[...] = _dot(h, w_ref[...]).astype(o_ref.dtype)


def _rms_matmul(x2d, g, w, tm=512, tn=1024):
    m, d = x2d.shape
    n = w.shape[1]
    return pl.pallas_call(
        _rms_matmul_kernel,
        grid=(m // tm, n // tn),
        in_specs=[pl.BlockSpec((tm, d), lambda i, j: (i, 0)),
                  pl.BlockSpec((1, d), lambda i, j: (0, 0)),
                  pl.BlockSpec((d, tn), lambda i, j: (0, j))],
        out_specs=pl.BlockSpec((tm, tn), lambda i, j: (i, j)),
        out_shape=jax.ShapeDtypeStruct((m, n), CDT),
        compiler_params=_cparams(("parallel", "parallel")),
        name="rms_matmul",
    )(x2d, g.reshape(1, d), w.astype(CDT))


def _stick_kernel(q_ref, k_ref, v_ref, o_ref):
    m = pl.program_id(2)
    scale = HEAD_DIM ** -0.5
    q = q_ref[0]
    per = SB_T // SB_G
    t_pos = m * SB_T + lax.broadcasted_iota(I32, (SB_T, 1), 0)
    lane = lax.broadcasted_iota(I32, (1, SB_T), 1)
    uj = lax.broadcasted_iota(I32, (SB_G, SB_G), 0)
    us = lax.broadcasted_iota(I32, (SB_G, SB_G), 1)
    suffix = jnp.where(uj >= us, 1.0, 0.0).astype(CDT)

    def head_chunk(hh, c, carry, masked):
        run, acc = carry
        rows = pl.ds(pl.multiple_of(c * SB_T, SB_T), SB_T)
        cols = slice(hh * HEAD_DIM, (hh + 1) * HEAD_DIM)
        z = _dot_nt(q[:, cols], k_ref[0, rows, cols]) * scale
        log_1m = -(jnp.maximum(z, 0.0) + jnp.log(1.0 + jnp.exp(-jnp.abs(z))))
        if masked:
            strict = c * SB_T + lane < t_pos
            log_1m = jnp.where(strict, log_1m, 0.0)
        stacked = jnp.concatenate([log_1m[:, g * SB_G:(g + 1) * SB_G] for g in range(per)], axis=0)
        sums = _dot(stacked.astype(CDT), suffix)
        later = [None] * per
        for g in range(per - 1, -1, -1):
            later[g] = run + sums[g * SB_T:(g + 1) * SB_T]
            run = run + sums[g * SB_T:(g + 1) * SB_T, 0:1]
        a = jnp.exp(z + jnp.concatenate(later, axis=1))
        if masked:
            a = jnp.where(strict, a, 0.0)
        acc = acc + _dot(a.astype(CDT), v_ref[0, rows, cols])
        return run, acc

    def chunk(c, carry, masked):
        return tuple(head_chunk(hh, c, carry[hh], masked) for hh in range(SB_HEADS))

    carry = tuple((jnp.zeros((SB_T, 1), F32), jnp.zeros((SB_T, HEAD_DIM), F32)) for _ in range(SB_HEADS))
    carry = chunk(m, carry, True)
    carry = lax.fori_loop(0, m, lambda i, cr: chunk(m - 1 - i, cr, False), carry)
    o_ref[0] = jnp.concatenate([cr[1] for cr in carry], axis=1).astype(o_ref.dtype)


def _stick(qkv, b, s):
    h = C_HEADS // SB_HEADS
    width = SB_HEADS * HEAD_DIM
    return pl.pallas_call(
        _stick_kernel,
        grid=(b, h, s // SB_T),
        in_specs=[pl.BlockSpec((1, SB_T, width), lambda i, j, k: (i, k, j)),
                  pl.BlockSpec((1, s, width), lambda i, j, k: (i, 0, h + j)),
                  pl.BlockSpec((1, s, width), lambda i, j, k: (i, 0, 2 * h + j))],
        out_specs=pl.BlockSpec((1, SB_T, width), lambda i, j, k: (i, k, j)),
        out_shape=jax.ShapeDtypeStruct((b, s, C_HEADS * HEAD_DIM), CDT),
        compiler_params=_cparams(("parallel", "parallel", "arbitrary")),
        name="stick_breaking",
    )(qkv, qkv, qkv)


def _ffn_kernel(*refs, final):
    it = iter(refs)
    x_ref, g_ref, wg_ref, wu_ref, wd_ref = next(it), next(it), next(it), next(it), next(it)
    fg_ref = next(it) if final else None
    o_ref, h_scr, acc_scr = next(it), next(it), next(it)
    f = pl.program_id(1)

    @pl.when(f == 0)
    def _():
        h_scr[...] = _rms(x_ref[...], g_ref[...]).astype(CDT)
        acc_scr[...] = jnp.zeros(acc_scr.shape, F32)

    h = h_scr[...]
    act = jax.nn.silu(_dot(h, wg_ref[...])) * _dot(h, wu_ref[...])
    acc_scr[...] += _dot(act.astype(CDT), wd_ref[...])

    @pl.when(f == pl.num_programs(1) - 1)
    def _():
        out = x_ref[...] + acc_scr[...]
        if final:
            out = _rms(out, fg_ref[...])
        o_ref[...] = out


def _ffn(x2d, g, w_gate, w_up, w_down, final_g=None, tm=512, tf=None):
    m, d = x2d.shape
    ff = w_gate.shape[1]
    final = final_g is not None
    row = pl.BlockSpec((tm, d), lambda i, f: (i, 0))
    vec = pl.BlockSpec((1, d), lambda i, f: (0, 0))
    ins = [x2d, g.reshape(1, d), w_gate.astype(CDT), w_up.astype(CDT), w_down.astype(CDT)]
    specs = [row, vec, pl.BlockSpec((d, tf), lambda i, f: (0, f)), pl.BlockSpec((d, tf), lambda i, f: (0, f)),
             pl.BlockSpec((tf, d), lambda i, f: (f, 0))]
    if final:
        ins.append(final_g.reshape(1, d))
        specs.append(vec)
    return pl.pallas_call(
        functools.partial(_ffn_kernel, final=final),
        grid=(m // tm, ff // tf),
        in_specs=specs,
        out_specs=row,
        out_shape=jax.ShapeDtypeStruct((m, d), F32),
        scratch_shapes=[pltpu.VMEM((tm, d), CDT), pltpu.VMEM((tm, d), F32)],
        compiler_params=_cparams(("parallel", "arbitrary")),
        name="swiglu",
    )(*ins)


MOE_TM = 512
MOE_TT = 512
RT_E1, RT_E2, RT_W1, RT_W2, RT_R1, RT_R2 = range(6)


def _router_kernel(x_ref, g_ref, wr_ref, h_ref, meta_ref, cnt_ref, base_scr):
    i = pl.program_id(0)

    @pl.when(i == 0)
    def _():
        base_scr[...] = jnp.zeros(base_scr.shape, F32)

    hn = _rms(x_ref[...], g_ref[...])
    h_ref[...] = hn
    logits = jnp.dot(hn, wr_ref[...], preferred_element_type=F32, precision=lax.Precision.HIGHEST)
    lane = lax.broadcasted_iota(I32, logits.shape, 1)
    logits = jnp.where(lane < N_EXPERTS, logits, -jnp.inf)
    v1 = jnp.max(logits, axis=1, keepdims=True)
    i1 = jnp.min(jnp.where(logits == v1, lane, LANES), axis=1, keepdims=True)
    rest = jnp.where(lane == i1, -jnp.inf, logits)
    v2 = jnp.max(rest, axis=1, keepdims=True)
    i2 = jnp.min(jnp.where(rest == v2, lane, LANES), axis=1, keepdims=True)
    e2 = jnp.exp(v2 - v1)
    den = 1.0 + e2
    oh1, oh2 = lane == i1, lane == i2
    hits = jnp.where(oh1 | oh2, 1.0, 0.0)
    tm = hits.shape[0]
    earlier = jnp.where(lax.broadcasted_iota(I32, (tm, tm), 1) < lax.broadcasted_iota(I32, (tm, tm), 0), 1.0, 0.0)
    before = _dot(earlier.astype(CDT), hits.astype(CDT)) + base_scr[...]
    r1 = jnp.sum(jnp.where(oh1, before, 0.0), axis=1, keepdims=True)
    r2 = jnp.sum(jnp.where(oh2, before, 0.0), axis=1, keepdims=True)
    base_scr[...] += jnp.sum(hits, axis=0, keepdims=True)
    cnt_ref[...] = base_scr[...]
    meta = jnp.zeros(logits.shape, F32)
    for k, v in ((RT_E1, i1.astype(F32)), (RT_E2, i2.astype(F32)), (RT_W1, 1.0 / den), (RT_W2, e2 / den),
                 (RT_R1, r1), (RT_R2, r2)):
        meta = jnp.where(lane == k, v, meta)
    meta_ref[...] = meta


def _router(x2d, g, w_router, tm=512):
    m, d = x2d.shape
    wr = jnp.concatenate([w_router, jnp.zeros((d, LANES - w_router.shape[1]), w_router.dtype)], axis=1)
    return pl.pallas_call(
        _router_kernel,
        grid=(m // tm,),
        in_specs=[pl.BlockSpec((tm, d), lambda i: (i, 0)), pl.BlockSpec((1, d), lambda i: (0, 0)),
                  pl.BlockSpec((d, LANES), lambda i: (0, 0))],
        out_specs=(pl.BlockSpec((tm, d), lambda i: (i, 0)), pl.BlockSpec((tm, LANES), lambda i: (i, 0)),
                   pl.BlockSpec((1, LANES), lambda i: (0, 0))),
        out_shape=(jax.ShapeDtypeStruct((m, d), F32), jax.ShapeDtypeStruct((m, LANES), F32),
                   jax.ShapeDtypeStruct((1, LANES), F32)),
        scratch_shapes=[pltpu.VMEM((1, LANES), F32)],
        compiler_params=_cparams(("arbitrary",)),
        name="moe_router",
    )(x2d, g.reshape(1, d), wr)


MOE_UNROLL = 8


def _row_copy(src, src_row, dst, dst_row, sem):
    return pltpu.make_async_copy(src.at[pl.ds(src_row, 1), :], dst.at[pl.ds(dst_row, 1), :], sem)


def _dispatch_kernel(p1_ref, p2_ref, h_ref, xs_in_hbm, xs_hbm, sem):
    del xs_in_hbm

    def start(r, carry):
        _row_copy(h_ref, r, xs_hbm, p1_ref[0, 0, r], sem).start()
        _row_copy(h_ref, r, xs_hbm, p2_ref[0, 0, r], sem).start()
        return carry

    def wait(r, carry):
        _row_copy(h_ref, 0, xs_hbm, 0, sem).wait()
        _row_copy(h_ref, 0, xs_hbm, 0, sem).wait()
        return carry

    lax.fori_loop(0, MOE_TT, start, 0, unroll=MOE_UNROLL)
    lax.fori_loop(0, MOE_TT, wait, 0, unroll=MOE_UNROLL)


def _dispatch(h, p1, p2, n_rows):
    m, d = h.shape
    tok = pl.BlockSpec((1, 1, MOE_TT), lambda i: (i, 0, 0), memory_space=pltpu.SMEM)
    anyspec = pl.BlockSpec(memory_space=pl.ANY)
    return pl.pallas_call(
        _dispatch_kernel,
        grid=(m // MOE_TT,),
        in_specs=[tok, tok, pl.BlockSpec((MOE_TT, d), lambda i: (i, 0)), anyspec],
        out_specs=anyspec,
        scratch_shapes=[pltpu.SemaphoreType.DMA(())],
        out_shape=jax.ShapeDtypeStruct((n_rows, d), F32),
        input_output_aliases={3: 0},
        compiler_params=_cparams(("arbitrary",)),
        name="moe_dispatch",
    )(p1, p2, h, jnp.zeros((n_rows, d), F32))


def _expert_ffn_kernel(te_ref, nu_ref, xs_ref, wg_ref, wu_ref, wd_ref, y_ref, h_scr, acc_scr):
    del te_ref
    i, f = pl.program_id(0), pl.program_id(1)
    used = i < nu_ref[0]

    @pl.when(used & (f == 0))
    def _():
        h_scr[...] = xs_ref[...].astype(CDT)
        acc_scr[...] = jnp.zeros(acc_scr.shape, F32)

    @pl.when(used)
    def _():
        h = h_scr[...]
        act = jax.nn.silu(_dot(h, wg_ref[0])) * _dot(h, wu_ref[0])
        acc_scr[...] += _dot(act.astype(CDT), wd_ref[0])

    @pl.when(f == pl.num_programs(1) - 1)
    def _():
        y_ref[...] = jnp.where(used, acc_scr[...], 0.0)


def _expert_ffn(xs, tile_expert, n_used, w_gate, w_up, w_down, tf):
    n_rows, d = xs.shape
    ff = w_gate.shape[2]
    row = pl.BlockSpec((MOE_TM, d), lambda i, f, te, nu: (i, 0))
    return pl.pallas_call(
        _expert_ffn_kernel,
        grid_spec=pltpu.PrefetchScalarGridSpec(
            num_scalar_prefetch=2, grid=(n_rows // MOE_TM, ff // tf),
            in_specs=[row,
                      pl.BlockSpec((1, d, tf), lambda i, f, te, nu: (te[i], 0, f)),
                      pl.BlockSpec((1, d, tf), lambda i, f, te, nu: (te[i], 0, f)),
                      pl.BlockSpec((1, tf, d), lambda i, f, te, nu: (te[i], f, 0))],
            out_specs=row,
            scratch_shapes=[pltpu.VMEM((MOE_TM, d), CDT), pltpu.VMEM((MOE_TM, d), F32)]),
        out_shape=jax.ShapeDtypeStruct((n_rows, d), F32),
        compiler_params=_cparams(("arbitrary", "arbitrary")),
        name="moe_expert_swiglu",
    )(tile_expert, n_used, xs, w_gate.astype(CDT), w_up.astype(CDT), w_down.astype(CDT))


def _combine_kernel(p1_ref, p2_ref, x_ref, meta_ref, fg_ref, y_hbm, o_ref, buf1, buf2, sem, *, final):
    def start(r, carry):
        _row_copy(y_hbm, p1_ref[0, 0, r], buf1, r, sem).start()
        _row_copy(y_hbm, p2_ref[0, 0, r], buf2, r, sem).start()
        return carry

    def wait(r, carry):
        _row_copy(y_hbm, 0, buf1, 0, sem).wait()
        _row_copy(y_hbm, 0, buf2, 0, sem).wait()
        return carry

    lax.fori_loop(0, MOE_TT, start, 0, unroll=MOE_UNROLL)
    lax.fori_loop(0, MOE_TT, wait, 0, unroll=MOE_UNROLL)
    meta = meta_ref[...]
    out = x_ref[...] + (meta[:, RT_W1:RT_W1 + 1] * buf1[...] + meta[:, RT_W2:RT_W2 + 1] * buf2[...])
    if final:
        out = _rms(out, fg_ref[...])
    o_ref[...] = out


def _combine(x2d, meta, y, p1, p2, final_g):
    m, d = x2d.shape
    final = final_g is not None
    fg = (final_g if final else jnp.ones((d,), F32)).reshape(1, d)
    tok = pl.BlockSpec((1, 1, MOE_TT), lambda i: (i, 0, 0), memory_space=pltpu.SMEM)
    return pl.pallas_call(
        functools.partial(_combine_kernel, final=final),
        grid=(m // MOE_TT,),
        in_specs=[tok, tok,
                  pl.BlockSpec((MOE_TT, d), lambda i: (i, 0)),
                  pl.BlockSpec((MOE_TT, LANES), lambda i: (i, 0)),
                  pl.BlockSpec((1, d), lambda i: (0, 0)),
                  pl.BlockSpec(memory_space=pl.ANY)],
        out_specs=pl.BlockSpec((MOE_TT, d), lambda i: (i, 0)),
        scratch_shapes=[pltpu.VMEM((MOE_TT, d), F32), pltpu.VMEM((MOE_TT, d), F32), pltpu.SemaphoreType.DMA(())],
        out_shape=jax.ShapeDtypeStruct((m, d), F32),
        compiler_params=_cparams(("arbitrary",)),
        name="moe_combine",
    )(p1, p2, x2d, meta, fg, y)


def _moe(x2d, g, w_router, w_gate, w_up, w_down, final_g, tf):
    m, d = x2d.shape
    n_e = w_gate.shape[0]
    h, meta, cnt = _router(x2d, g, w_router)
    n_tiles = (2 * m) // MOE_TM + n_e
    experts = jnp.arange(n_e, dtype=I32)
    padded = ((cnt[0, :n_e].astype(I32) + MOE_TM - 1) // MOE_TM) * MOE_TM
    ends = jnp.sum(jnp.where(experts[None, :] <= experts[:, None], padded[None, :], 0), axis=1)
    off = ends - padded
    tile_start = jnp.arange(n_tiles, dtype=I32) * MOE_TM
    tile_expert = jnp.minimum(jnp.sum((tile_start[:, None] >= ends[None, :]).astype(I32), axis=1), n_e - 1)
    n_used = (ends[n_e - 1:] // MOE_TM).astype(I32)

    def position(e_lane, r_lane):
        e = meta[:, e_lane].astype(I32)
        seg = jnp.sum(jnp.where(e[:, None] == experts[None, :], off[None, :], 0), axis=1)
        return (seg + meta[:, r_lane].astype(I32)).reshape(m // MOE_TT, 1, MOE_TT)

    p1, p2 = position(RT_E1, RT_R1), position(RT_E2, RT_R2)
    xs = _dispatch(h, p1, p2, n_tiles * MOE_TM)
    y = _expert_ffn(xs, tile_expert, n_used, w_gate, w_up, w_down, tf)
    return _combine(x2d, meta, y, p1, p2, final_g)


def _even_mixer(x2d, b, s, g, w_in, kv_norm, w_uk, w_uv, cmp_pos, cmp_w1, cmp_w2, w_out, tokb, tokbt, cmpb):
    (aqt, iqt, smallt, ak, avt, ik, bq, ckr, cvr, sk, sv, wk, wv, small) = _even_inproj(
        x2d, g, w_in, kv_norm, w_uk, w_uv)
    r3 = lambda a: a.reshape(b, s, a.shape[1])
    oa = _dsa(aqt, iqt, smallt, r3(ik), r3(ak), avt, tokbt)
    cmp = _compress(jnp.stack([ckr, cvr]).reshape(2, b, s, HEAD_DIM), cmp_pos, cmp_w1, cmp_w2)
    ob = _nsa(r3(bq), r3(small), cmp[0], cmp[1], r3(sk), r3(sv), r3(wk), r3(wv), tokb, cmpb)
    return _proj_resid([oa.reshape(b * s, EV_AQ), ob.reshape(b * s, EV_BQ)],
                       [w_out[:EV_AQ], w_out[EV_AQ:]], x2d)


def kernel(x, rel_bias, ev_norm_mix, ev_w_in, ev_kv_norm, ev_w_uk, ev_w_uv, ev_cmp_pos, ev_cmp_w1, ev_cmp_w2, ev_w_out, ev_norm_ffn, ev_w_gate, ev_w_up, ev_w_down, od_norm_mix, od_w_qkv, od_w_out, od_norm_ffn, od_w_router, od_w_gate, od_w_up, od_w_down, final_norm):
    b, s, d = x.shape
    depth = ev_norm_mix.shape[0] + od_norm_mix.shape[0]
    tokb, tokbt, cmpb = _bias_tiles(rel_bias)
    x2d = x.reshape(b * s, d)
    for layer in range(depth):
        i = layer // 2
        last = layer == depth - 1
        if layer % 2 == 0:
            x2d = _even_mixer(x2d, b, s, ev_norm_mix[i], ev_w_in[i], ev_kv_norm[i], ev_w_uk[i], ev_w_uv[i],
                              ev_cmp_pos[i], ev_cmp_w1[i], ev_cmp_w2[i], ev_w_out[i], tokb, tokbt, cmpb)
            x2d = _ffn(x2d, ev_norm_ffn[i], ev_w_gate[i], ev_w_up[i], ev_w_down[i],
                       final_g=final_norm if last else None, tf=ev_w_gate.shape[2] // 2)
        else:
            qkv = _rms_matmul(x2d, od_norm_mix[i], od_w_qkv[i])
            o = _stick(qkv.reshape(b, s, qkv.shape[1]), b, s)
            x2d = _proj_resid([o.reshape(b * s, o.shape[2])], [od_w_out[i]], x2d)
            x2d = _moe(x2d, od_norm_ffn[i], od_w_router[i], od_w_gate[i], od_w_up[i], od_w_down[i],
                       final_norm if last else None, tf=od_w_gate.shape[3] // 4)
    return x2d.reshape(b, s, d)
```

```python
import functools
import math

import numpy as np
import jax
import jax.numpy as jnp
from jax import lax
from jax.experimental import pallas as pl
from jax.experimental.pallas import tpu as pltpu

HEAD_DIM = 128
A_HEADS = 4
A_KV_RANK = 256
IDX_HEADS = 4
IDX_DIM = 64
DSA_TOPK = 256
B_HEADS = 4
CMP_LEN = 32
CMP_STRIDE = 16
SEL_BLOCK = 64
N_SEL = 8
WINDOW = 512
C_HEADS = 8
NUM_BUCKETS = 32
MAX_DISTANCE = 128
N_EXPERTS = 8
RMS_EPS = 1e-6
NEG_BIG = -1e30

LANES = 128
CDT = jnp.bfloat16
F32 = jnp.float32
I32 = jnp.int32
INT_MIN = -2 ** 31
KEY_NEG_INF = -2 ** 31 + 0x7FFFFF
VMEM_LIMIT = 56 * 1024 * 1024

TQ = 256
KC = 256
SB_G = 256
SB_T = 512
SB_HEADS = 4


def _cparams(sem):
    return pltpu.CompilerParams(dimension_semantics=sem, vmem_limit_bytes=VMEM_LIMIT)


def _dot(a, b):
    return jnp.dot(a, b, preferred_element_type=F32)


def _dot_nt(a, b):
    return lax.dot_general(a, b, (((1,), (1,)), ((), ())), preferred_element_type=F32)


def _rms(x, g):
    return x * lax.rsqrt(jnp.mean(x * x, axis=-1, keepdims=True) + RMS_EPS) * g


def _split_terms(x, n):
    if CDT == F32:
        return [x]
    out, r = [], x
    for _ in range(n):
        h = r.astype(CDT)
        out.append(h)
        r = r - h.astype(F32)
    return out


def _sortable(v):
    bits = lax.bitcast_convert_type(v, I32)
    key = bits ^ ((bits >> 31) & 0x7FFFFFFF)
    return jnp.where(v == 0.0, 0, key)


def _bucket_np(dist):
    n = np.maximum(dist, 0)
    max_exact = NUM_BUCKETS // 2
    large = max_exact + (np.log(np.maximum(n, 1).astype(np.float32) / np.float32(max_exact))
                         / np.float32(math.log(MAX_DISTANCE / max_exact))
                         * np.float32(NUM_BUCKETS - max_exact)).astype(np.int32)
    large = np.minimum(large, NUM_BUCKETS - 1)
    return np.where(n < max_exact, n, large).astype(np.int32)


def _bias_tile_kernel(bias_ref, tokbk_ref, cmpbk_ref, tok_ref, cmp_ref, *, n_heads):
    tb = tokbk_ref[...]
    cb = cmpbk_ref[...]
    for h in range(n_heads):
        far = bias_ref[NUM_BUCKETS - 1, h]
        tacc = jnp.zeros(tb.shape, F32)
        cacc = jnp.zeros(cb.shape, F32)
        for k in range(NUM_BUCKETS - 1):
            v = bias_ref[k, h] - far
            tacc = jnp.where(tb == k, v, tacc)
            cacc = jnp.where(cb == k, v, cacc)
        tok_ref[:, h * TQ:(h + 1) * TQ] = tacc
        cmp_ref[:, h * TQ:(h + 1) * TQ] = cacc


def _bias_tiles(rel_bias):
    n_heads = rel_bias.shape[1]
    i = np.arange(TQ)[None, :]
    j = np.arange(2 * KC)[:, None]
    dist = i + KC - j
    tokbk = np.where(dist >= 0, _bucket_np(dist), NUM_BUCKETS - 1).astype(np.int32)
    r = np.arange(LANES)[:, None]
    cdist = i + (TQ - CMP_LEN + 1) - CMP_STRIDE * r
    cmpbk = np.where((cdist >= 0) & (r < 2 * TQ // CMP_STRIDE), _bucket_np(cdist), NUM_BUCKETS - 1).astype(np.int32)
    return pl.pallas_call(
        functools.partial(_bias_tile_kernel, n_heads=n_heads),
        out_shape=(jax.ShapeDtypeStruct((2 * KC, n_heads * TQ), F32),
                   jax.ShapeDtypeStruct((LANES, n_heads * TQ), F32)),
        in_specs=[pl.BlockSpec(memory_space=pltpu.SMEM)] + [pl.BlockSpec(memory_space=pltpu.VMEM)] * 2,
        out_specs=(pl.BlockSpec(memory_space=pltpu.VMEM),) * 2,
        compiler_params=pltpu.CompilerParams(vmem_limit_bytes=VMEM_LIMIT),
        name="bias_tiles",
    )(rel_bias, jnp.asarray(tokbk), jnp.asarray(cmpbk))


EV_AQ = A_HEADS * HEAD_DIM
EV_IQ = IDX_HEADS * IDX_DIM
EV_BQ = B_HEADS * HEAD_DIM
EV_OFF_LAT = 0
EV_OFF_KV = EV_OFF_LAT + A_KV_RANK
EV_OFF_IK = EV_OFF_KV + 4 * HEAD_DIM
EV_TOFF_AQ = 0
EV_TOFF_IQ = EV_TOFF_AQ + EV_AQ
EV_TOFF_SMALL = EV_TOFF_IQ + EV_IQ
EV_TOFF_BQ = EV_TOFF_SMALL + LANES
EV_TOFF_V = EV_TOFF_BQ + EV_BQ
SM_IK = 0
SM_IW = IDX_DIM
SM_G = IDX_DIM + IDX_HEADS


def _even_inproj_kernel(x_ref, g_ref, w_ref, wt_ref, kvg_ref, wuk_ref, wuvt_ref,
                        aqt_ref, iqt_ref, smallt_ref, bqt_ref, svt_ref, wvt_ref, avt_ref,
                        ak_ref, ik_ref, ckr_ref, cvr_ref, sk_ref, wk_ref):
    h = _rms(x_ref[...], g_ref[...]).astype(CDT)

    def seg(off, width):
        return _dot(h, w_ref[:, off:off + width])

    def seg_t(off, width):
        return _dot_nt(wt_ref[off:off + width, :], h)

    aqt_ref[...] = seg_t(EV_TOFF_AQ, EV_AQ).astype(aqt_ref.dtype)
    iqt_ref[...] = seg_t(EV_TOFF_IQ, EV_IQ).astype(iqt_ref.dtype)
    smallt_ref[...] = seg_t(EV_TOFF_SMALL, LANES)
    bqt_ref[...] = seg_t(EV_TOFF_BQ, EV_BQ).astype(bqt_ref.dtype)
    v_t = seg_t(EV_TOFF_V, 2 * HEAD_DIM)
    svt_ref[...] = v_t[:HEAD_DIM].astype(svt_ref.dtype)
    wvt_ref[...] = v_t[HEAD_DIM:].astype(wvt_ref.dtype)
    lat = seg(EV_OFF_LAT, A_KV_RANK)
    ckv = _rms(lat, kvg_ref[...]).astype(CDT)
    ak_ref[...] = _dot(ckv, wuk_ref[...]).astype(ak_ref.dtype)
    avt_ref[...] = _dot_nt(wuvt_ref[...], ckv).astype(avt_ref.dtype)
    kv = seg(EV_OFF_KV, 4 * HEAD_DIM)
    for j, ref in enumerate((ckr_ref, cvr_ref, sk_ref, wk_ref)):
        ref[...] = kv[:, j * HEAD_DIM:(j + 1) * HEAD_DIM].astype(ref.dtype)
    ik_ref[...] = seg(EV_OFF_IK, LANES)[:, :IDX_DIM].astype(ik_ref.dtype)


def _even_inproj(x2d, g, w_in, kv_norm, w_uk, w_uv, tm=512):
    m, d = x2d.shape
    offs = np.cumsum((EV_AQ, A_KV_RANK, EV_IQ, IDX_DIM, IDX_HEADS, EV_BQ, 6 * HEAD_DIM, 3 * B_HEADS))
    aq, lat, iq, ik, iw, bq, bkv, bg = jnp.split(w_in, [int(o) for o in offs[:-1]], axis=1)
    ckr, cvr, sk, sv, wk, wv = jnp.split(bkv, 6, axis=1)
    zeros = lambda n: jnp.zeros((d, n), w_in.dtype)
    small_w = jnp.concatenate([ik, iw, bg, zeros(LANES - IDX_DIM - IDX_HEADS - 3 * B_HEADS)], axis=1)
    w = jnp.concatenate([lat, ckr, cvr, sk, wk, ik, zeros(LANES - IDX_DIM)], axis=1).astype(CDT)
    wt = jnp.concatenate([aq, iq, small_w, bq, sv, wv], axis=1).T.astype(CDT)
    row = lambda width: pl.BlockSpec((tm, width), lambda i: (i, 0))
    col = lambda height: pl.BlockSpec((height, tm), lambda i: (0, i))
    full = lambda a: pl.BlockSpec(a.shape, lambda i: (0,) * a.ndim)
    g2, kvg2 = g.reshape(1, d), kv_norm.reshape(1, A_KV_RANK)
    wuk, wuvt = w_uk.astype(CDT), w_uv.T.astype(CDT)
    t_outs = ((EV_AQ, CDT), (EV_IQ, CDT), (LANES, F32), (EV_BQ, CDT), (HEAD_DIM, CDT), (HEAD_DIM, CDT), (HEAD_DIM, CDT))
    r_outs = ((HEAD_DIM, CDT), (IDX_DIM, CDT), (HEAD_DIM, F32), (HEAD_DIM, F32), (HEAD_DIM, CDT), (HEAD_DIM, CDT))
    return pl.pallas_call(
        _even_inproj_kernel,
        grid=(m // tm,),
        in_specs=[row(d), full(g2), full(w), full(wt), full(kvg2), full(wuk), full(wuvt)],
        out_specs=tuple([col(ht) for ht, _ in t_outs] + [row(wd) for wd, _ in r_outs]),
        out_shape=tuple([jax.ShapeDtypeStruct((ht, m), dt) for ht, dt in t_outs]
                        + [jax.ShapeDtypeStruct((m, wd), dt) for wd, dt in r_outs]),
        compiler_params=_cparams(("parallel",)),
        name="even_inproj",
    )(x2d, g2, w, wt, kvg2, wuk, wuvt)


def _compress_kernel(kv_ref, pos_ref, w1_ref, w2_ref, o_ref):
    kv = kv_ref[0, 0]
    half = kv.shape[1]
    first = _dot((kv + pos_ref[0, :, :half]).astype(CDT), w1_ref[0, :half, :])
    second = _dot((kv + pos_ref[0, :, half:]).astype(CDT), w1_ref[0, half:, :])
    hid = first + pltpu.roll(second, second.shape[0] - 1, 0)
    hid = jax.nn.gelu(hid)
    o_ref[0, 0] = _dot(hid.astype(CDT), w2_ref[0]).astype(o_ref.dtype)


def _compress(raw, cmp_pos, cmp_w1, cmp_w2):
    _, b, s, hd = raw.shape
    g = s // CMP_STRIDE
    kv = raw.reshape(2, b, g, CMP_STRIDE * hd)
    pos = cmp_pos.reshape(2, 1, CMP_LEN * hd)
    return pl.pallas_call(
        _compress_kernel,
        grid=(2, b),
        in_specs=[pl.BlockSpec((1, 1, g, CMP_STRIDE * hd), lambda j, i: (j, i, 0, 0)),
                  pl.BlockSpec((1, 1, CMP_LEN * hd), lambda j, i: (j, 0, 0)),
                  pl.BlockSpec((1, CMP_LEN * hd, hd), lambda j, i: (j, 0, 0)),
                  pl.BlockSpec((1, hd, hd), lambda j, i: (j, 0, 0))],
        out_specs=pl.BlockSpec((1, 1, g, hd), lambda j, i: (j, i, 0, 0)),
        out_shape=jax.ShapeDtypeStruct((2, b, g, hd), CDT),
        compiler_params=_cparams(("parallel", "parallel")),
        name="nsa_compress",
    )(kv, pos, cmp_w1.astype(CDT), cmp_w2.astype(CDT))


def _softmax_step(carry, logits, mask, v_t):
    m_i, l_i, acc = carry
    s = jnp.where(mask, logits, NEG_BIG)
    m_new = jnp.maximum(m_i, jnp.max(s, axis=0, keepdims=True))
    alpha = jnp.exp(m_i - m_new)
    p = jnp.where(mask, jnp.exp(s - m_new), 0.0)
    l_new = alpha * l_i + jnp.sum(p, axis=0, keepdims=True)
    acc = alpha * acc + _dot(v_t, p.astype(CDT))
    return m_new, l_new, acc


def _softmax_init(lanes):
    return (jnp.full((1, lanes), NEG_BIG, F32), jnp.zeros((1, lanes), F32), jnp.zeros((HEAD_DIM, lanes), F32))


def _softmax_out(carry):
    _, l_i, acc = carry
    return jnp.where(l_i > 0.0, acc / jnp.where(l_i > 0.0, l_i, 1.0), 0.0)


def _tile_heads(x, n):
    return jnp.concatenate([x] * n, axis=1)


def _heads_to_rows(o_t, n):
    return jnp.concatenate([o_t[:, h * TQ:(h + 1) * TQ].T for h in range(n)], axis=1)


def _dsa_kernel(aqt_ref, iqt_ref, smallt_ref, ik_ref, ak_ref, avt_ref, tokbt_ref, o_ref, key_scr, j_scr,
                *, seq, topk):
    m = pl.program_id(1)
    scale = HEAD_DIM ** -0.5
    lanes4 = A_HEADS * TQ
    q_t = jnp.concatenate([aqt_ref[h * HEAD_DIM:(h + 1) * HEAD_DIM, :] for h in range(A_HEADS)], axis=1)
    iq_t = jnp.concatenate([iqt_ref[h * IDX_DIM:(h + 1) * IDX_DIM, :] for h in range(IDX_HEADS)], axis=1)
    iw = smallt_ref[SM_IW:SM_IW + IDX_HEADS, :] * (IDX_HEADS ** -0.5)
    t_pos = m * TQ + lax.broadcasted_iota(I32, (1, TQ), 1)
    kidx = lax.broadcasted_iota(I32, (KC, 1), 0)

    @pl.when(m == 0)
    def _():
        key_scr[...] = jnp.full(key_scr.shape, INT_MIN, I32)

    def chunk(c):
        return pl.ds(pl.multiple_of(c * KC, KC), KC)

    def score_body(c, carry):
        s = _dot(ik_ref[0, chunk(c), :], iq_t)
        isc = jnp.zeros((KC, TQ), F32)
        for h in range(IDX_HEADS):
            isc = isc + jnp.maximum(s[:, h * TQ:(h + 1) * TQ] * (IDX_DIM ** -0.5), 0.0) * iw[h:h + 1, :]
        isc = jnp.where(c * KC + kidx <= t_pos, isc, -jnp.inf)
        key_scr[chunk(c), :] = _sortable(isc)
        return carry

    lax.fori_loop(0, m + 1, score_body, 0)

    def count(pred):
        def body(c, acc):
            hit = jnp.where(pred(key_scr[chunk(c), :], c * KC + kidx), 1.0, 0.0)
            return acc + jnp.sum(hit.reshape(KC // 8, 8, TQ), axis=0)
        acc = lax.fori_loop(0, m + 1, body, jnp.zeros((8, TQ), F32))
        return jnp.sum(acc, axis=0, keepdims=True)

    kf = float(topk)
    t0 = jnp.where(count(lambda k, _: k >= 0) >= kf, 0, INT_MIN).astype(I32)

    def bit_body(i, t_cur):
        cand = t_cur | lax.shift_left(jnp.int32(1), 30 - i)
        return jnp.where(count(lambda k, _: k >= cand) >= kf, cand, t_cur)

    thr = lax.fori_loop(0, 31, bit_body, t0)
    n_ge = count(lambda k, _: k >= thr)
    n_gt = count(lambda k, _: k > thr)
    need = kf - n_gt
    tied = (n_ge > kf) & (thr > KEY_NEG_INF)
    j_scr[...] = jnp.full(j_scr.shape, seq, I32)

    @pl.when(jnp.max(jnp.where(tied, 1.0, 0.0)) > 0.0)
    def _():
        def jbit(i, j_cur):
            cand = j_cur | lax.shift_left(jnp.int32(1), (seq.bit_length() - 2) - i)
            return jnp.where(count(lambda k, idx: (k == thr) & (idx < cand)) < need, cand, j_cur)
        j_found = lax.fori_loop(0, seq.bit_length() - 1, jbit, jnp.zeros((1, TQ), I32))
        j_scr[...] = jnp.broadcast_to(jnp.where(tied, j_found, seq), j_scr.shape)

    j_last = j_scr[0:1, :]

    def attend(c, carry, bias, causal):
        logits = _dot(ak_ref[0, chunk(c), :], q_t) * scale
        if bias is not None:
            logits = logits + bias
        k = key_scr[chunk(c), :]
        idx = c * KC + kidx
        sel = (k > thr) | ((k == thr) & (idx <= j_last))
        if causal:
            sel = sel & (idx <= t_pos)
        return _softmax_step(carry, logits, _tile_heads(sel, A_HEADS), avt_ref[:, chunk(c)])

    carry = _softmax_init(lanes4)
    carry = lax.fori_loop(0, jnp.maximum(m - 1, 0), lambda c, cr: attend(c, cr, None, False), carry)
    carry = lax.cond(m >= 1, lambda cr: attend(m - 1, cr, tokbt_ref[:KC, :], False), lambda cr: cr, carry)
    carry = attend(m, carry, tokbt_ref[KC:, :], True)
    o_ref[0] = _heads_to_rows(_softmax_out(carry), A_HEADS).astype(o_ref.dtype)


def _dsa(aqt, iqt, smallt, ik, ak, avt, tokbt):
    b, s, _ = ak.shape
    nq = s // TQ
    topk = min(DSA_TOPK, s // 4)
    qspec = lambda rows: pl.BlockSpec((rows, TQ), lambda i, j: (0, i * nq + j))
    kspec = lambda w: pl.BlockSpec((1, s, w), lambda i, j: (i, 0, 0))
    return pl.pallas_call(
        functools.partial(_dsa_kernel, seq=s, topk=topk),
        grid=(b, nq),
        in_specs=[qspec(EV_AQ), qspec(EV_IQ), qspec(LANES), kspec(IDX_DIM), kspec(HEAD_DIM),
                  pl.BlockSpec((HEAD_DIM, s), lambda i, j: (0, i)),
                  pl.BlockSpec((2 * KC, A_HEADS * TQ), lambda i, j: (0, 0))],
        out_specs=pl.BlockSpec((1, TQ, EV_AQ), lambda i, j: (i, j, 0)),
        out_shape=jax.ShapeDtypeStruct((b, s, EV_AQ), CDT),
        scratch_shapes=[pltpu.VMEM((s, TQ), I32), pltpu.VMEM((8, TQ), I32)],
        compiler_params=_cparams(("parallel", "arbitrary")),
        name="dsa_attention",
    )(aqt, iqt, smallt, ik, ak, avt, tokbt)


def _nsa_kernel(bqt_ref, smallt_ref, ck_ref, cv_ref, sk_ref, svt_ref, wk_ref, wvt_ref, tokbt_ref, cmpbt_ref, o_ref,
                *, seq):
    m = pl.program_id(1)
    scale = HEAD_DIM ** -0.5
    n_c = seq // CMP_STRIDE
    n_sb = seq // SEL_BLOCK
    lanes4 = B_HEADS * TQ
    sel_shift = SEL_BLOCK.bit_length() - 1
    q_t = jnp.concatenate([bqt_ref[h * HEAD_DIM:(h + 1) * HEAD_DIM, :] for h in range(B_HEADS)], axis=1)
    t_pos = m * TQ + lax.broadcasted_iota(I32, (1, TQ), 1)
    kidx = lax.broadcasted_iota(I32, (KC, 1), 0)

    def chunk(c):
        return pl.ds(pl.multiple_of(c * KC, KC), KC)

    lc = _dot(ck_ref[0], q_t) * scale
    n_i = lax.broadcasted_iota(I32, (n_c, LANES), 0)
    r_i = lax.broadcasted_iota(I32, (n_c, LANES), 1)
    first_n = m * (TQ // CMP_STRIDE) - TQ // CMP_STRIDE
    place = jnp.where(n_i == first_n + r_i, 1.0, 0.0).astype(CDT)
    terms = _split_terms(cmpbt_ref[...], 2)
    lc = lc + _dot(jnp.concatenate([place] * len(terms), axis=1), jnp.concatenate(terms, axis=0))
    n_col = lax.broadcasted_iota(I32, (n_c, 1), 0)
    cvalid = (n_col * CMP_STRIDE + (CMP_LEN - 1) <= t_pos) & (n_col < n_c - 1)
    cv4 = _tile_heads(cvalid, B_HEADS)
    sc = jnp.where(cv4, lc, NEG_BIG)
    pc = jnp.where(cv4, jnp.exp(sc - jnp.max(sc, axis=0, keepdims=True)), 0.0)
    den = jnp.sum(pc, axis=0, keepdims=True)
    pc = jnp.where(den > 0.0, pc / jnp.where(den > 0.0, den, 1.0), 0.0)
    cv_t = cv_ref[0].astype(F32).T.astype(CDT)
    oc = _dot(cv_t, pc.astype(CDT))

    psum = pc[:, 0:TQ]
    for h in range(1, B_HEADS):
        psum = psum + pc[:, h * TQ:(h + 1) * TQ]
    om = lax.broadcasted_iota(I32, (LANES, n_c), 0) * SEL_BLOCK
    on = lax.broadcasted_iota(I32, (LANES, n_c), 1) * CMP_STRIDE
    overlap = jnp.where((on < om + SEL_BLOCK) & (on + CMP_LEN > om) & (om < seq), 1.0, 0.0).astype(CDT)
    terms = _split_terms(psum, 3)
    imp = _dot(jnp.concatenate([overlap] * len(terms), axis=1), jnp.concatenate(terms, axis=0))
    blk = lax.broadcasted_iota(I32, (LANES, 1), 0)
    cur = t_pos >> sel_shift
    forced = (blk == 0) | (blk == cur) | (blk == cur - 1)
    val = jnp.where(forced, jnp.inf, jnp.where(blk > cur, -jnp.inf, imp))
    blk_f = blk.astype(F32)
    alive = blk < n_sb
    sel = jnp.zeros((LANES, TQ), F32)
    for _ in range(min(N_SEL, n_sb)):
        best = jnp.max(jnp.where(alive, val, -jnp.inf), axis=0, keepdims=True)
        first = jnp.min(jnp.where(alive & (val == best), blk_f, float(LANES)), axis=0, keepdims=True)
        pick = blk_f == first
        sel = jnp.where(pick, 1.0, sel)
        alive = alive & jnp.logical_not(pick)
    sel_c = sel.astype(CDT)

    def near_bias(near):
        return tokbt_ref[near * KC:(near + 1) * KC, :]

    def sel_attend(c, carry, near):
        logits = _dot(sk_ref[0, chunk(c), :], q_t) * scale
        e_key = lax.broadcasted_iota(I32, (KC, LANES), 0)
        e_blk = lax.broadcasted_iota(I32, (KC, LANES), 1)
        expand = jnp.where(e_blk == c * (KC // SEL_BLOCK) + (e_key >> sel_shift), 1.0, 0.0).astype(CDT)
        mask = _dot(expand, sel_c) > 0.5
        if near is not None:
            logits = logits + near_bias(near)
            mask = mask & (c * KC + kidx <= t_pos)
        return _softmax_step(carry, logits, _tile_heads(mask, B_HEADS), svt_ref[:, chunk(c)])

    cs = _softmax_init(lanes4)
    cs = lax.fori_loop(0, jnp.maximum(m - 1, 0), lambda c, cr: sel_attend(c, cr, None), cs)
    cs = lax.cond(m >= 1, lambda cr: sel_attend(m - 1, cr, 0), lambda cr: cr, cs)
    cs = sel_attend(m, cs, 1)
    osel = _softmax_out(cs)

    def win_attend(c, carry, near):
        logits = _dot(wk_ref[0, chunk(c), :], q_t) * scale
        dist = t_pos - (c * KC + kidx)
        mask = (dist >= 0) & (dist < WINDOW)
        if near is not None:
            logits = logits + near_bias(near)
        return _softmax_step(carry, logits, _tile_heads(mask, B_HEADS), wvt_ref[:, chunk(c)])

    cw = _softmax_init(lanes4)
    for back in range(WINDOW // KC, 1, -1):
        cw = lax.cond(m >= back, lambda cr, back=back: win_attend(m - back, cr, None), lambda cr: cr, cw)
    cw = lax.cond(m >= 1, lambda cr: win_attend(m - 1, cr, 0), lambda cr: cr, cw)
    cw = win_attend(m, cw, 1)
    ow = _softmax_out(cw)

    g0 = SM_G - SM_IW
    gate = jax.nn.sigmoid(smallt_ref[SM_IW:SM_G + 3 * B_HEADS, :])
    outs = []
    for h in range(B_HEADS):
        cols = slice(h * TQ, (h + 1) * TQ)
        outs.append(gate[g0 + h:g0 + h + 1] * oc[:, cols]
                    + gate[g0 + B_HEADS + h:g0 + B_HEADS + h + 1] * osel[:, cols]
                    + gate[g0 + 2 * B_HEADS + h:g0 + 2 * B_HEADS + h + 1] * ow[:, cols])
    o_ref[0] = _heads_to_rows(jnp.concatenate(outs, axis=1), B_HEADS).astype(o_ref.dtype)


def _nsa(bqt, smallt, ck, cv, sk, svt, wk, wvt, tokbt, cmpbt):
    b, s, _ = sk.shape
    nq = s // TQ
    n_c = s // CMP_STRIDE
    qspec = lambda rows: pl.BlockSpec((rows, TQ), lambda i, j: (0, i * nq + j))
    kspec = lambda n: pl.BlockSpec((1, n, HEAD_DIM), lambda i, j: (i, 0, 0))
    vspec = pl.BlockSpec((HEAD_DIM, s), lambda i, j: (0, i))
    return pl.pallas_call(
        functools.partial(_nsa_kernel, seq=s),
        grid=(b, nq),
        in_specs=[qspec(EV_BQ), qspec(LANES), kspec(n_c), kspec(n_c), kspec(s), vspec, kspec(s), vspec,
                  pl.BlockSpec((2 * KC, B_HEADS * TQ), lambda i, j: (0, 1)),
                  pl.BlockSpec((LANES, B_HEADS * TQ), lambda i, j: (0, 1))],
        out_specs=pl.BlockSpec((1, TQ, EV_BQ), lambda i, j: (i, j, 0)),
        out_shape=jax.ShapeDtypeStruct((b, s, EV_BQ), CDT),
        compiler_params=_cparams(("parallel", "arbitrary")),
        name="nsa_attention",
    )(bqt, smallt, ck, cv, sk, svt, wk, wvt, tokbt, cmpbt)


def _proj_resid_kernel(*refs, n_in):
    a_refs, w_refs, r_ref, o_ref = refs[:n_in], refs[n_in:2 * n_in], refs[2 * n_in], refs[2 * n_in + 1]
    acc = r_ref[...]
    for a_ref, w_ref in zip(a_refs, w_refs):
        acc = acc + _dot(a_ref[...], w_ref[...])
    o_ref[...] = acc


def _proj_resid(acts, weights, resid, tm=512):
    m, d = resid.shape
    row = lambda width: pl.BlockSpec((tm, width), lambda i: (i, 0))
    full = lambda a: pl.BlockSpec(a.shape, lambda i: (0, 0))
    weights = [w.astype(CDT) for w in weights]
    return pl.pallas_call(
        functools.partial(_proj_resid_kernel, n_in=len(acts)),
        grid=(m // tm,),
        in_specs=[row(a.shape[1]) for a in acts] + [full(w) for w in weights] + [row(d)],
        out_specs=row(d),
        out_shape=jax.ShapeDtypeStruct((m, d), F32),
        compiler_params=_cparams(("parallel",)),
        name="proj_residual",
    )(*acts, *weights, resid)


def _rms_matmul_kernel(x_ref, g_ref, w_ref, o_ref):
    h = _rms(x_ref[...], g_ref[...]).astype(CDT)
    o_ref[...] = _dot(h, w_ref[...]).astype(o_ref.dtype)


def _rms_matmul(x2d, g, w, tm=512, tn=1024):
    m, d = x2d.shape
    n = w.shape[1]
    return pl.pallas_call(
        _rms_matmul_kernel,
        grid=(m // tm, n // tn),
        in_specs=[pl.BlockSpec((tm, d), lambda i, j: (i, 0)),
                  pl.BlockSpec((1, d), lambda i, j: (0, 0)),
                  pl.BlockSpec((d, tn), lambda i, j: (0, j))],
        out_specs=pl.BlockSpec((tm, tn), lambda i, j: (i, j)),
        out_shape=jax.ShapeDtypeStruct((m, n), CDT),
        compiler_params=_cparams(("parallel", "parallel")),
        name="rms_matmul",
    )(x2d, g.reshape(1, d), w.astype(CDT))


def _stick_kernel(q_ref, k_ref, v_ref, o_ref):
    m = pl.program_id(2)
    scale = HEAD_DIM ** -0.5 * math.log2(math.e)
    q = q_ref[0]
    per = SB_T // SB_G
    t_pos = m * SB_T + lax.broadcasted_iota(I32, (SB_T, 1), 0)
    lane = lax.broadcasted_iota(I32, (1, SB_T), 1)
    uj = lax.broadcasted_iota(I32, (SB_G, SB_G), 0)
    us = lax.broadcasted_iota(I32, (SB_G, SB_G), 1)
    suffix = jnp.where(uj >= us, 1.0, 0.0).astype(CDT)

    def head_chunk(hh, c, carry, masked):
        run, acc = carry
        rows = pl.ds(pl.multiple_of(c * SB_T, SB_T), SB_T)
        cols = slice(hh * HEAD_DIM, (hh + 1) * HEAD_DIM)
        z = _dot_nt(q[:, cols], k_ref[0, rows, cols]) * scale
        neg_z = -z
        log_1m = jnp.minimum(neg_z, 0.0) - jnp.log2(1.0 + jnp.exp2(jnp.minimum(z, neg_z)))
        if masked:
            strict = c * SB_T + lane < t_pos
            log_1m = jnp.where(strict, log_1m, 0.0)
        stacked = jnp.concatenate([log_1m[:, g * SB_G:(g + 1) * SB_G] for g in range(per)], axis=0)
        sums = _dot(stacked.astype(CDT), suffix)
        later = [None] * per
        for g in range(per - 1, -1, -1):
            later[g] = run + sums[g * SB_T:(g + 1) * SB_T]
            run = run + sums[g * SB_T:(g + 1) * SB_T, 0:1]
        a = jnp.exp2(z + jnp.concatenate(later, axis=1))
        if masked:
            a = jnp.where(strict, a, 0.0)
        acc = acc + _dot(a.astype(CDT), v_ref[0, rows, cols])
        return run, acc

    def chunk(c, carry, masked):
        return tuple(head_chunk(hh, c, carry[hh], masked) for hh in range(SB_HEADS))

    carry = tuple((jnp.zeros((SB_T, 1), F32), jnp.zeros((SB_T, HEAD_DIM), F32)) for _ in range(SB_HEADS))
    carry = chunk(m, carry, True)
    carry = lax.fori_loop(0, m, lambda i, cr: chunk(m - 1 - i, cr, False), carry)
    o_ref[0] = jnp.concatenate([cr[1] for cr in carry], axis=1).astype(o_ref.dtype)


def _stick(qkv, b, s):
    h = C_HEADS // SB_HEADS
    width = SB_HEADS * HEAD_DIM
    return pl.pallas_call(
        _stick_kernel,
        grid=(b, h, s // SB_T),
        in_specs=[pl.BlockSpec((1, SB_T, width), lambda i, j, k: (i, k, j)),
                  pl.BlockSpec((1, s, width), lambda i, j, k: (i, 0, h + j)),
                  pl.BlockSpec((1, s, width), lambda i, j, k: (i, 0, 2 * h + j))],
        out_specs=pl.BlockSpec((1, SB_T, width), lambda i, j, k: (i, k, j)),
        out_shape=jax.ShapeDtypeStruct((b, s, C_HEADS * HEAD_DIM), CDT),
        compiler_params=_cparams(("parallel", "parallel", "arbitrary")),
        name="stick_breaking",
    )(qkv, qkv, qkv)


def _ffn_kernel(*refs, final):
    it = iter(refs)
    x_ref, g_ref, wg_ref, wu_ref, wd_ref = next(it), next(it), next(it), next(it), next(it)
    fg_ref = next(it) if final else None
    o_ref, h_scr, acc_scr = next(it), next(it), next(it)
    f = pl.program_id(1)

    @pl.when(f == 0)
    def _():
        h_scr[...] = _rms(x_ref[...], g_ref[...]).astype(CDT)
        acc_scr[...] = jnp.zeros(acc_scr.shape, F32)

    h = h_scr[...]
    act = jax.nn.silu(_dot(h, wg_ref[...])) * _dot(h, wu_ref[...])
    acc_scr[...] += _dot(act.astype(CDT), wd_ref[...])

    @pl.when(f == pl.num_programs(1) - 1)
    def _():
        out = x_ref[...] + acc_scr[...]
        if final:
            out = _rms(out, fg_ref[...])
        o_ref[...] = out


def _ffn(x2d, g, w_gate, w_up, w_down, final_g=None, tm=512, tf=None):
    m, d = x2d.shape
    ff = w_gate.shape[1]
    final = final_g is not None
    row = pl.BlockSpec((tm, d), lambda i, f: (i, 0))
    vec = pl.BlockSpec((1, d), lambda i, f: (0, 0))
    ins = [x2d, g.reshape(1, d), w_gate.astype(CDT), w_up.astype(CDT), w_down.astype(CDT)]
    specs = [row, vec, pl.BlockSpec((d, tf), lambda i, f: (0, f)), pl.BlockSpec((d, tf), lambda i, f: (0, f)),
             pl.BlockSpec((tf, d), lambda i, f: (f, 0))]
    if final:
        ins.append(final_g.reshape(1, d))
        specs.append(vec)
    return pl.pallas_call(
        functools.partial(_ffn_kernel, final=final),
        grid=(m // tm, ff // tf),
        in_specs=specs,
        out_specs=row,
        out_shape=jax.ShapeDtypeStruct((m, d), F32),
        scratch_shapes=[pltpu.VMEM((tm, d), CDT), pltpu.VMEM((tm, d), F32)],
        compiler_params=_cparams(("parallel", "arbitrary")),
        name="swiglu",
    )(*ins)


MOE_TM = 512
MOE_TT = 512
RT_E1, RT_E2, RT_W1, RT_W2, RT_R1, RT_R2 = range(6)


def _router_kernel(x_ref, g_ref, wr_ref, h_ref, meta_ref, cnt_ref, base_scr):
    i = pl.program_id(0)

    @pl.when(i == 0)
    def _():
        base_scr[...] = jnp.zeros(base_scr.shape, F32)

    hn = _rms(x_ref[...], g_ref[...])
    h_ref[...] = hn
    logits = jnp.dot(hn, wr_ref[...], preferred_element_type=F32, precision=lax.Precision.HIGHEST)
    lane = lax.broadcasted_iota(I32, logits.shape, 1)
    logits = jnp.where(lane < N_EXPERTS, logits, -jnp.inf)
    v1 = jnp.max(logits, axis=1, keepdims=True)
    i1 = jnp.min(jnp.where(logits == v1, lane, LANES), axis=1, keepdims=True)
    rest = jnp.where(lane == i1, -jnp.inf, logits)
    v2 = jnp.max(rest, axis=1, keepdims=True)
    i2 = jnp.min(jnp.where(rest == v2, lane, LANES), axis=1, keepdims=True)
    e2 = jnp.exp(v2 - v1)
    den = 1.0 + e2
    oh1, oh2 = lane == i1, lane == i2
    hits = jnp.where(oh1 | oh2, 1.0, 0.0)
    tm = hits.shape[0]
    earlier = jnp.where(lax.broadcasted_iota(I32, (tm, tm), 1) < lax.broadcasted_iota(I32, (tm, tm), 0), 1.0, 0.0)
    before = _dot(earlier.astype(CDT), hits.astype(CDT)) + base_scr[...]
    r1 = jnp.sum(jnp.where(oh1, before, 0.0), axis=1, keepdims=True)
    r2 = jnp.sum(jnp.where(oh2, before, 0.0), axis=1, keepdims=True)
    base_scr[...] += jnp.sum(hits, axis=0, keepdims=True)
    cnt_ref[...] = base_scr[...]
    meta = jnp.zeros(logits.shape, F32)
    for k, v in ((RT_E1, i1.astype(F32)), (RT_E2, i2.astype(F32)), (RT_W1, 1.0 / den), (RT_W2, e2 / den),
                 (RT_R1, r1), (RT_R2, r2)):
        meta = jnp.where(lane == k, v, meta)
    meta_ref[...] = meta


def _router(x2d, g, w_router, tm=512):
    m, d = x2d.shape
    wr = jnp.concatenate([w_router, jnp.zeros((d, LANES - w_router.shape[1]), w_router.dtype)], axis=1)
    return pl.pallas_call(
        _router_kernel,
        grid=(m // tm,),
        in_specs=[pl.BlockSpec((tm, d), lambda i: (i, 0)), pl.BlockSpec((1, d), lambda i: (0, 0)),
                  pl.BlockSpec((d, LANES), lambda i: (0, 0))],
        out_specs=(pl.BlockSpec((tm, d), lambda i: (i, 0)), pl.BlockSpec((tm, LANES), lambda i: (i, 0)),
                   pl.BlockSpec((1, LANES), lambda i: (0, 0))),
        out_shape=(jax.ShapeDtypeStruct((m, d), F32), jax.ShapeDtypeStruct((m, LANES), F32),
                   jax.ShapeDtypeStruct((1, LANES), F32)),
        scratch_shapes=[pltpu.VMEM((1, LANES), F32)],
        compiler_params=_cparams(("arbitrary",)),
        name="moe_router",
    )(x2d, g.reshape(1, d), wr)


MOE_UNROLL = 8


def _row_copy(src, src_row, dst, dst_row, sem):
    return pltpu.make_async_copy(src.at[pl.ds(src_row, 1), :], dst.at[pl.ds(dst_row, 1), :], sem)


def _dispatch_kernel(p1_ref, p2_ref, h_ref, xs_in_hbm, xs_hbm, sem):
    del xs_in_hbm

    def start(r, carry):
        _row_copy(h_ref, r, xs_hbm, p1_ref[0, 0, r], sem).start()
        _row_copy(h_ref, r, xs_hbm, p2_ref[0, 0, r], sem).start()
        return carry

    def wait(r, carry):
        _row_copy(h_ref, 0, xs_hbm, 0, sem).wait()
        _row_copy(h_ref, 0, xs_hbm, 0, sem).wait()
        return carry

    lax.fori_loop(0, MOE_TT, start, 0, unroll=MOE_UNROLL)
    lax.fori_loop(0, MOE_TT, wait, 0, unroll=MOE_UNROLL)


def _dispatch(h, p1, p2, n_rows):
    m, d = h.shape
    tok = pl.BlockSpec((1, 1, MOE_TT), lambda i: (i, 0, 0), memory_space=pltpu.SMEM)
    anyspec = pl.BlockSpec(memory_space=pl.ANY)
    return pl.pallas_call(
        _dispatch_kernel,
        grid=(m // MOE_TT,),
        in_specs=[tok, tok, pl.BlockSpec((MOE_TT, d), lambda i: (i, 0)), anyspec],
        out_specs=anyspec,
        scratch_shapes=[pltpu.SemaphoreType.DMA(())],
        out_shape=jax.ShapeDtypeStruct((n_rows, d), F32),
        input_output_aliases={3: 0},
        compiler_params=_cparams(("arbitrary",)),
        name="moe_dispatch",
    )(p1, p2, h, jnp.zeros((n_rows, d), F32))


def _expert_ffn_kernel(te_ref, nu_ref, xs_ref, wg_ref, wu_ref, wd_ref, y_ref, h_scr, acc_scr):
    del te_ref
    i, f = pl.program_id(0), pl.program_id(1)
    used = i < nu_ref[0]

    @pl.when(used & (f == 0))
    def _():
        h_scr[...] = xs_ref[...].astype(CDT)
        acc_scr[...] = jnp.zeros(acc_scr.shape, F32)

    @pl.when(used)
    def _():
        h = h_scr[...]
        act = jax.nn.silu(_dot(h, wg_ref[0])) * _dot(h, wu_ref[0])
        acc_scr[...] += _dot(act.astype(CDT), wd_ref[0])

    @pl.when(f == pl.num_programs(1) - 1)
    def _():
        y_ref[...] = jnp.where(used, acc_scr[...], 0.0)


def _expert_ffn(xs, tile_expert, n_used, w_gate, w_up, w_down, tf):
    n_rows, d = xs.shape
    ff = w_gate.shape[2]
    row = pl.BlockSpec((MOE_TM, d), lambda i, f, te, nu: (i, 0))
    return pl.pallas_call(
        _expert_ffn_kernel,
        grid_spec=pltpu.PrefetchScalarGridSpec(
            num_scalar_prefetch=2, grid=(n_rows // MOE_TM, ff // tf),
            in_specs=[row,
                      pl.BlockSpec((1, d, tf), lambda i, f, te, nu: (te[i], 0, f)),
                      pl.BlockSpec((1, d, tf), lambda i, f, te, nu: (te[i], 0, f)),
                      pl.BlockSpec((1, tf, d), lambda i, f, te, nu: (te[i], f, 0))],
            out_specs=row,
            scratch_shapes=[pltpu.VMEM((MOE_TM, d), CDT), pltpu.VMEM((MOE_TM, d), F32)]),
        out_shape=jax.ShapeDtypeStruct((n_rows, d), F32),
        compiler_params=_cparams(("arbitrary", "arbitrary")),
        name="moe_expert_swiglu",
    )(tile_expert, n_used, xs, w_gate.astype(CDT), w_up.astype(CDT), w_down.astype(CDT))


def _combine_kernel(p1_ref, p2_ref, x_ref, meta_ref, fg_ref, y_hbm, o_ref, buf1, buf2, sem, *, final):
    def start(r, carry):
        _row_copy(y_hbm, p1_ref[0, 0, r], buf1, r, sem).start()
        _row_copy(y_hbm, p2_ref[0, 0, r], buf2, r, sem).start()
        return carry

    def wait(r, carry):
        _row_copy(y_hbm, 0, buf1, 0, sem).wait()
        _row_copy(y_hbm, 0, buf2, 0, sem).wait()
        return carry

    lax.fori_loop(0, MOE_TT, start, 0, unroll=MOE_UNROLL)
    lax.fori_loop(0, MOE_TT, wait, 0, unroll=MOE_UNROLL)
    meta = meta_ref[...]
    out = x_ref[...] + (meta[:, RT_W1:RT_W1 + 1] * buf1[...] + meta[:, RT_W2:RT_W2 + 1] * buf2[...])
    if final:
        out = _rms(out, fg_ref[...])
    o_ref[...] = out


def _combine(x2d, meta, y, p1, p2, final_g):
    m, d = x2d.shape
    final = final_g is not None
    fg = (final_g if final else jnp.ones((d,), F32)).reshape(1, d)
    tok = pl.BlockSpec((1, 1, MOE_TT), lambda i: (i, 0, 0), memory_space=pltpu.SMEM)
    return pl.pallas_call(
        functools.partial(_combine_kernel, final=final),
        grid=(m // MOE_TT,),
        in_specs=[tok, tok,
                  pl.BlockSpec((MOE_TT, d), lambda i: (i, 0)),
                  pl.BlockSpec((MOE_TT, LANES), lambda i: (i, 0)),
                  pl.BlockSpec((1, d), lambda i: (0, 0)),
                  pl.BlockSpec(memory_space=pl.ANY)],
        out_specs=pl.BlockSpec((MOE_TT, d), lambda i: (i, 0)),
        scratch_shapes=[pltpu.VMEM((MOE_TT, d), F32), pltpu.VMEM((MOE_TT, d), F32), pltpu.SemaphoreType.DMA(())],
        out_shape=jax.ShapeDtypeStruct((m, d), F32),
        compiler_params=_cparams(("arbitrary",)),
        name="moe_combine",
    )(p1, p2, x2d, meta, fg, y)


def _moe(x2d, g, w_router, w_gate, w_up, w_down, final_g, tf):
    m, d = x2d.shape
    n_e = w_gate.shape[0]
    h, meta, cnt = _router(x2d, g, w_router)
    n_tiles = (2 * m) // MOE_TM + n_e
    experts = jnp.arange(n_e, dtype=I32)
    padded = ((cnt[0, :n_e].astype(I32) + MOE_TM - 1) // MOE_TM) * MOE_TM
    ends = jnp.sum(jnp.where(experts[None, :] <= experts[:, None], padded[None, :], 0), axis=1)
    off = ends - padded
    tile_start = jnp.arange(n_tiles, dtype=I32) * MOE_TM
    tile_expert = jnp.minimum(jnp.sum((tile_start[:, None] >= ends[None, :]).astype(I32), axis=1), n_e - 1)
    n_used = (ends[n_e - 1:] // MOE_TM).astype(I32)

    def position(e_lane, r_lane):
        e = meta[:, e_lane].astype(I32)
        seg = jnp.sum(jnp.where(e[:, None] == experts[None, :], off[None, :], 0), axis=1)
        return (seg + meta[:, r_lane].astype(I32)).reshape(m // MOE_TT, 1, MOE_TT)

    p1, p2 = position(RT_E1, RT_R1), position(RT_E2, RT_R2)
    xs = _dispatch(h, p1, p2, n_tiles * MOE_TM)
    y = _expert_ffn(xs, tile_expert, n_used, w_gate, w_up, w_down, tf)
    return _combine(x2d, meta, y, p1, p2, final_g)


def _even_mixer(x2d, b, s, g, w_in, kv_norm, w_uk, w_uv, cmp_pos, cmp_w1, cmp_w2, w_out, tokbt, cmpbt):
    (aqt, iqt, smallt, bqt, svt, wvt, avt, ak, ik, ckr, cvr, sk, wk) = _even_inproj(
        x2d, g, w_in, kv_norm, w_uk, w_uv)
    r3 = lambda a: a.reshape(b, s, a.shape[1])
    oa = _dsa(aqt, iqt, smallt, r3(ik), r3(ak), avt, tokbt)
    cmp = _compress(jnp.stack([ckr, cvr]).reshape(2, b, s, HEAD_DIM), cmp_pos, cmp_w1, cmp_w2)
    ob = _nsa(bqt, smallt, cmp[0], cmp[1], r3(sk), svt, r3(wk), wvt, tokbt, cmpbt)
    return _proj_resid([oa.reshape(b * s, EV_AQ), ob.reshape(b * s, EV_BQ)],
                       [w_out[:EV_AQ], w_out[EV_AQ:]], x2d)


def kernel(x, rel_bias, ev_norm_mix, ev_w_in, ev_kv_norm, ev_w_uk, ev_w_uv, ev_cmp_pos, ev_cmp_w1, ev_cmp_w2, ev_w_out, ev_norm_ffn, ev_w_gate, ev_w_up, ev_w_down, od_norm_mix, od_w_qkv, od_w_out, od_norm_ffn, od_w_router, od_w_gate, od_w_up, od_w_down, final_norm):
    b, s, d = x.shape
    depth = ev_norm_mix.shape[0] + od_norm_mix.shape[0]
    tokbt, cmpbt = _bias_tiles(rel_bias)
    x2d = x.reshape(b * s, d)
    for layer in range(depth):
        i = layer // 2
        last = layer == depth - 1
        if layer % 2 == 0:
            x2d = _even_mixer(x2d, b, s, ev_norm_mix[i], ev_w_in[i], ev_kv_norm[i], ev_w_uk[i], ev_w_uv[i],
                              ev_cmp_pos[i], ev_cmp_w1[i], ev_cmp_w2[i], ev_w_out[i], tokbt, cmpbt)
            x2d = _ffn(x2d, ev_norm_ffn[i], ev_w_gate[i], ev_w_up[i], ev_w_down[i],
                       final_g=final_norm if last else None, tf=ev_w_gate.shape[2] // 2)
        else:
            qkv = _rms_matmul(x2d, od_norm_mix[i], od_w_qkv[i])
            o = _stick(qkv.reshape(b, s, qkv.shape[1]), b, s)
            x2d = _proj_resid([o.reshape(b * s, o.shape[2])], [od_w_out[i]], x2d)
            x2d = _moe(x2d, od_norm_ffn[i], od_w_router[i], od_w_gate[i], od_w_up[i], od_w_down[i],
                       final_norm if last else None, tf=od_w_gate.shape[3] // 2)
    return x2d.reshape(b, s, d)
```

```python
import functools
import math

import numpy as np
import jax
import jax.numpy as jnp
from jax import lax
from jax.experimental import pallas as pl
from jax.experimental.pallas import tpu as pltpu

HEAD_DIM = 128
A_HEADS = 4
A_KV_RANK = 256
IDX_HEADS = 4
IDX_DIM = 64
DSA_TOPK = 256
B_HEADS = 4
CMP_LEN = 32
CMP_STRIDE = 16
SEL_BLOCK = 64
N_SEL = 8
WINDOW = 512
C_HEADS = 8
NUM_BUCKETS = 32
MAX_DISTANCE = 128
N_EXPERTS = 8
RMS_EPS = 1e-6
NEG_BIG = -1e30

LANES = 128
CDT = jnp.bfloat16
F32 = jnp.float32
I32 = jnp.int32
INT_MIN = -2 ** 31
KEY_NEG_INF = -2 ** 31 + 0x7FFFFF
VMEM_LIMIT = 56 * 1024 * 1024

TQ = 256
KC = 256
SB_G = 256
SB_T = 512
SB_HEADS = 4
SB_DEAD = -256.0


def _cparams(sem):
    return pltpu.CompilerParams(dimension_semantics=sem, vmem_limit_bytes=VMEM_LIMIT)


def _dot(a, b):
    return jnp.dot(a, b, preferred_element_type=F32)


def _dot_nt(a, b):
    return lax.dot_general(a, b, (((1,), (1,)), ((), ())), preferred_element_type=F32)


def _rms(x, g):
    return x * lax.rsqrt(jnp.mean(x * x, axis=-1, keepdims=True) + RMS_EPS) * g


def _split_terms(x, n):
    if CDT == F32:
        return [x]
    out, r = [], x
    for _ in range(n):
        h = r.astype(CDT)
        out.append(h)
        r = r - h.astype(F32)
    return out


def _sortable(v):
    bits = lax.bitcast_convert_type(v, I32)
    key = bits ^ ((bits >> 31) & 0x7FFFFFFF)
    return jnp.where(v == 0.0, 0, key)


def _bucket_np(dist):
    n = np.maximum(dist, 0)
    max_exact = NUM_BUCKETS // 2
    large = max_exact + (np.log(np.maximum(n, 1).astype(np.float32) / np.float32(max_exact))
                         / np.float32(math.log(MAX_DISTANCE / max_exact))
                         * np.float32(NUM_BUCKETS - max_exact)).astype(np.int32)
    large = np.minimum(large, NUM_BUCKETS - 1)
    return np.where(n < max_exact, n, large).astype(np.int32)


def _bias_tile_kernel(bias_ref, tokbk_ref, cmpbk_ref, tok_ref, cmp_ref, *, n_heads):
    tb = tokbk_ref[...]
    cb = cmpbk_ref[...]
    for h in range(n_heads):
        far = bias_ref[NUM_BUCKETS - 1, h]
        tacc = jnp.zeros(tb.shape, F32)
        cacc = jnp.zeros(cb.shape, F32)
        for k in range(NUM_BUCKETS - 1):
            v = bias_ref[k, h] - far
            tacc = jnp.where(tb == k, v, tacc)
            cacc = jnp.where(cb == k, v, cacc)
        tok_ref[:, h * TQ:(h + 1) * TQ] = tacc
        cmp_ref[:, h * TQ:(h + 1) * TQ] = cacc


def _bias_tiles(rel_bias):
    n_heads = rel_bias.shape[1]
    i = np.arange(TQ)[None, :]
    j = np.arange(2 * KC)[:, None]
    dist = i + KC - j
    tokbk = np.where(dist >= 0, _bucket_np(dist), NUM_BUCKETS - 1).astype(np.int32)
    r = np.arange(LANES)[:, None]
    cdist = i + (TQ - CMP_LEN + 1) - CMP_STRIDE * r
    cmpbk = np.where((cdist >= 0) & (r < 2 * TQ // CMP_STRIDE), _bucket_np(cdist), NUM_BUCKETS - 1).astype(np.int32)
    return pl.pallas_call(
        functools.partial(_bias_tile_kernel, n_heads=n_heads),
        out_shape=(jax.ShapeDtypeStruct((2 * KC, n_heads * TQ), F32),
                   jax.ShapeDtypeStruct((LANES, n_heads * TQ), F32)),
        in_specs=[pl.BlockSpec(memory_space=pltpu.SMEM)] + [pl.BlockSpec(memory_space=pltpu.VMEM)] * 2,
        out_specs=(pl.BlockSpec(memory_space=pltpu.VMEM),) * 2,
        compiler_params=pltpu.CompilerParams(vmem_limit_bytes=VMEM_LIMIT),
        name="bias_tiles",
    )(rel_bias, jnp.asarray(tokbk), jnp.asarray(cmpbk))


EV_AQ = A_HEADS * HEAD_DIM
EV_IQ = IDX_HEADS * IDX_DIM
EV_BQ = B_HEADS * HEAD_DIM
EV_OFF_LAT = 0
EV_OFF_KV = EV_OFF_LAT + A_KV_RANK
EV_OFF_IK = EV_OFF_KV + 4 * HEAD_DIM
EV_TOFF_AQ = 0
EV_TOFF_IQ = EV_TOFF_AQ + EV_AQ
EV_TOFF_SMALL = EV_TOFF_IQ + EV_IQ
EV_TOFF_BQ = EV_TOFF_SMALL + LANES
EV_TOFF_V = EV_TOFF_BQ + EV_BQ
SM_IK = 0
SM_IW = IDX_DIM
SM_G = IDX_DIM + IDX_HEADS


def _even_inproj_kernel(x_ref, g_ref, w_ref, wt_ref, kvg_ref, wuk_ref, wuvt_ref,
                        aqt_ref, iqt_ref, smallt_ref, bqt_ref, svt_ref, wvt_ref, avt_ref,
                        ak_ref, ik_ref, ckr_ref, cvr_ref, sk_ref, wk_ref):
    h = _rms(x_ref[...], g_ref[...]).astype(CDT)

    def seg(off, width):
        return _dot(h, w_ref[:, off:off + width])

    def seg_t(off, width):
        return _dot_nt(wt_ref[off:off + width, :], h)

    aqt_ref[...] = seg_t(EV_TOFF_AQ, EV_AQ).astype(aqt_ref.dtype)
    iqt_ref[...] = seg_t(EV_TOFF_IQ, EV_IQ).astype(iqt_ref.dtype)
    smallt_ref[...] = seg_t(EV_TOFF_SMALL, LANES)
    bqt_ref[...] = seg_t(EV_TOFF_BQ, EV_BQ).astype(bqt_ref.dtype)
    v_t = seg_t(EV_TOFF_V, 2 * HEAD_DIM)
    svt_ref[...] = v_t[:HEAD_DIM].astype(svt_ref.dtype)
    wvt_ref[...] = v_t[HEAD_DIM:].astype(wvt_ref.dtype)
    lat = seg(EV_OFF_LAT, A_KV_RANK)
    ckv = _rms(lat, kvg_ref[...]).astype(CDT)
    ak_ref[...] = _dot(ckv, wuk_ref[...]).astype(ak_ref.dtype)
    avt_ref[...] = _dot_nt(wuvt_ref[...], ckv).astype(avt_ref.dtype)
    kv = seg(EV_OFF_KV, 4 * HEAD_DIM)
    for j, ref in enumerate((ckr_ref, cvr_ref, sk_ref, wk_ref)):
        ref[...] = kv[:, j * HEAD_DIM:(j + 1) * HEAD_DIM].astype(ref.dtype)
    ik_ref[...] = seg(EV_OFF_IK, LANES)[:, :IDX_DIM].astype(ik_ref.dtype)


def _even_inproj(x2d, g, w_in, kv_norm, w_uk, w_uv, tm=512):
    m, d = x2d.shape
    offs = np.cumsum((EV_AQ, A_KV_RANK, EV_IQ, IDX_DIM, IDX_HEADS, EV_BQ, 6 * HEAD_DIM, 3 * B_HEADS))
    aq, lat, iq, ik, iw, bq, bkv, bg = jnp.split(w_in, [int(o) for o in offs[:-1]], axis=1)
    ckr, cvr, sk, sv, wk, wv = jnp.split(bkv, 6, axis=1)
    zeros = lambda n: jnp.zeros((d, n), w_in.dtype)
    small_w = jnp.concatenate([ik, iw, bg, zeros(LANES - IDX_DIM - IDX_HEADS - 3 * B_HEADS)], axis=1)
    w = jnp.concatenate([lat, ckr, cvr, sk, wk, ik, zeros(LANES - IDX_DIM)], axis=1).astype(CDT)
    wt = jnp.concatenate([aq, iq, small_w, bq, sv, wv], axis=1).T.astype(CDT)
    row = lambda width: pl.BlockSpec((tm, width), lambda i: (i, 0))
    col = lambda height: pl.BlockSpec((height, tm), lambda i: (0, i))
    full = lambda a: pl.BlockSpec(a.shape, lambda i: (0,) * a.ndim)
    g2, kvg2 = g.reshape(1, d), kv_norm.reshape(1, A_KV_RANK)
    wuk, wuvt = w_uk.astype(CDT), w_uv.T.astype(CDT)
    t_outs = ((EV_AQ, CDT), (EV_IQ, CDT), (LANES, F32), (EV_BQ, CDT), (HEAD_DIM, CDT), (HEAD_DIM, CDT), (HEAD_DIM, CDT))
    r_outs = ((HEAD_DIM, CDT), (IDX_DIM, CDT), (HEAD_DIM, F32), (HEAD_DIM, F32), (HEAD_DIM, CDT), (HEAD_DIM, CDT))
    return pl.pallas_call(
        _even_inproj_kernel,
        grid=(m // tm,),
        in_specs=[row(d), full(g2), full(w), full(wt), full(kvg2), full(wuk), full(wuvt)],
        out_specs=tuple([col(ht) for ht, _ in t_outs] + [row(wd) for wd, _ in r_outs]),
        out_shape=tuple([jax.ShapeDtypeStruct((ht, m), dt) for ht, dt in t_outs]
                        + [jax.ShapeDtypeStruct((m, wd), dt) for wd, dt in r_outs]),
        compiler_params=_cparams(("parallel",)),
        name="even_inproj",
    )(x2d, g2, w, wt, kvg2, wuk, wuvt)


def _compress_kernel(kv_ref, pos_ref, w1_ref, w2_ref, o_ref):
    kv = kv_ref[0, 0]
    half = kv.shape[1]
    first = _dot((kv + pos_ref[0, :, :half]).astype(CDT), w1_ref[0, :half, :])
    second = _dot((kv + pos_ref[0, :, half:]).astype(CDT), w1_ref[0, half:, :])
    hid = first + pltpu.roll(second, second.shape[0] - 1, 0)
    hid = jax.nn.gelu(hid)
    o_ref[0, 0] = _dot(hid.astype(CDT), w2_ref[0]).astype(o_ref.dtype)


def _compress(raw, cmp_pos, cmp_w1, cmp_w2):
    _, b, s, hd = raw.shape
    g = s // CMP_STRIDE
    kv = raw.reshape(2, b, g, CMP_STRIDE * hd)
    pos = cmp_pos.reshape(2, 1, CMP_LEN * hd)
    return pl.pallas_call(
        _compress_kernel,
        grid=(2, b),
        in_specs=[pl.BlockSpec((1, 1, g, CMP_STRIDE * hd), lambda j, i: (j, i, 0, 0)),
                  pl.BlockSpec((1, 1, CMP_LEN * hd), lambda j, i: (j, 0, 0)),
                  pl.BlockSpec((1, CMP_LEN * hd, hd), lambda j, i: (j, 0, 0)),
                  pl.BlockSpec((1, hd, hd), lambda j, i: (j, 0, 0))],
        out_specs=pl.BlockSpec((1, 1, g, hd), lambda j, i: (j, i, 0, 0)),
        out_shape=jax.ShapeDtypeStruct((2, b, g, hd), CDT),
        compiler_params=_cparams(("parallel", "parallel")),
        name="nsa_compress",
    )(kv, pos, cmp_w1.astype(CDT), cmp_w2.astype(CDT))


def _softmax_step(carry, logits, mask, v_t):
    m_i, l_i, acc = carry
    s = jnp.where(mask, logits, NEG_BIG)
    m_new = jnp.maximum(m_i, jnp.max(s, axis=0, keepdims=True))
    alpha = jnp.exp(m_i - m_new)
    p = jnp.where(mask, jnp.exp(s - m_new), 0.0)
    l_new = alpha * l_i + jnp.sum(p, axis=0, keepdims=True)
    acc = alpha * acc + _dot(v_t, p.astype(CDT))
    return m_new, l_new, acc


def _softmax_init(lanes):
    return (jnp.full((1, lanes), NEG_BIG, F32), jnp.zeros((1, lanes), F32), jnp.zeros((HEAD_DIM, lanes), F32))


def _softmax_out(carry):
    _, l_i, acc = carry
    return jnp.where(l_i > 0.0, acc / jnp.where(l_i > 0.0, l_i, 1.0), 0.0)


def _tile_heads(x, n):
    return jnp.concatenate([x] * n, axis=1)


def _heads_to_rows(o_t, n):
    return jnp.concatenate([o_t[:, h * TQ:(h + 1) * TQ].T for h in range(n)], axis=1)


def _dsa_kernel(aqt_ref, iqt_ref, smallt_ref, ik_ref, ak_ref, avt_ref, tokbt_ref, o_ref, key_scr, j_scr,
                *, seq, topk):
    m = pl.program_id(1)
    scale = HEAD_DIM ** -0.5
    lanes4 = A_HEADS * TQ
    q_t = jnp.concatenate([aqt_ref[h * HEAD_DIM:(h + 1) * HEAD_DIM, :] for h in range(A_HEADS)], axis=1)
    iq_t = jnp.concatenate([iqt_ref[h * IDX_DIM:(h + 1) * IDX_DIM, :] for h in range(IDX_HEADS)], axis=1)
    iw = smallt_ref[SM_IW:SM_IW + IDX_HEADS, :] * (IDX_HEADS ** -0.5)
    t_pos = m * TQ + lax.broadcasted_iota(I32, (1, TQ), 1)
    kidx = lax.broadcasted_iota(I32, (KC, 1), 0)

    @pl.when(m == 0)
    def _():
        key_scr[...] = jnp.full(key_scr.shape, INT_MIN, I32)

    def chunk(c):
        return pl.ds(pl.multiple_of(c * KC, KC), KC)

    def score_body(c, carry):
        s = _dot(ik_ref[0, chunk(c), :], iq_t)
        isc = jnp.zeros((KC, TQ), F32)
        for h in range(IDX_HEADS):
            isc = isc + jnp.maximum(s[:, h * TQ:(h + 1) * TQ] * (IDX_DIM ** -0.5), 0.0) * iw[h:h + 1, :]
        isc = jnp.where(c * KC + kidx <= t_pos, isc, -jnp.inf)
        key_scr[chunk(c), :] = _sortable(isc)
        return carry

    lax.fori_loop(0, m + 1, score_body, 0)

    def count(pred):
        def body(c, acc):
            hit = jnp.where(pred(key_scr[chunk(c), :], c * KC + kidx), 1.0, 0.0)
            return acc + jnp.sum(hit.reshape(KC // 8, 8, TQ), axis=0)
        acc = lax.fori_loop(0, m + 1, body, jnp.zeros((8, TQ), F32))
        return jnp.sum(acc, axis=0, keepdims=True)

    kf = float(topk)
    t0 = jnp.where(count(lambda k, _: k >= 0) >= kf, 0, INT_MIN).astype(I32)

    def bit_body(i, t_cur):
        cand = t_cur | lax.shift_left(jnp.int32(1), 30 - i)
        return jnp.where(count(lambda k, _: k >= cand) >= kf, cand, t_cur)

    thr = lax.fori_loop(0, 31, bit_body, t0)
    n_ge = count(lambda k, _: k >= thr)
    n_gt = count(lambda k, _: k > thr)
    need = kf - n_gt
    tied = (n_ge > kf) & (thr > KEY_NEG_INF)
    j_scr[...] = jnp.full(j_scr.shape, seq, I32)

    @pl.when(jnp.max(jnp.where(tied, 1.0, 0.0)) > 0.0)
    def _():
        def jbit(i, j_cur):
            cand = j_cur | lax.shift_left(jnp.int32(1), (seq.bit_length() - 2) - i)
            return jnp.where(count(lambda k, idx: (k == thr) & (idx < cand)) < need, cand, j_cur)
        j_found = lax.fori_loop(0, seq.bit_length() - 1, jbit, jnp.zeros((1, TQ), I32))
        j_scr[...] = jnp.broadcast_to(jnp.where(tied, j_found, seq), j_scr.shape)

    j_last = j_scr[0:1, :]

    def attend(c, carry, bias, causal):
        logits = _dot(ak_ref[0, chunk(c), :], q_t) * scale
        if bias is not None:
            logits = logits + bias
        k = key_scr[chunk(c), :]
        idx = c * KC + kidx
        sel = (k > thr) | ((k == thr) & (idx <= j_last))
        if causal:
            sel = sel & (idx <= t_pos)
        return _softmax_step(carry, logits, _tile_heads(sel, A_HEADS), avt_ref[:, chunk(c)])

    carry = _softmax_init(lanes4)
    carry = lax.fori_loop(0, jnp.maximum(m - 1, 0), lambda c, cr: attend(c, cr, None, False), carry)
    carry = lax.cond(m >= 1, lambda cr: attend(m - 1, cr, tokbt_ref[:KC, :], False), lambda cr: cr, carry)
    carry = attend(m, carry, tokbt_ref[KC:, :], True)
    o_ref[0] = _heads_to_rows(_softmax_out(carry), A_HEADS).astype(o_ref.dtype)


def _dsa(aqt, iqt, smallt, ik, ak, avt, tokbt):
    b, s, _ = ak.shape
    nq = s // TQ
    topk = min(DSA_TOPK, s // 4)
    qspec = lambda rows: pl.BlockSpec((rows, TQ), lambda i, j: (0, i * nq + j))
    kspec = lambda w: pl.BlockSpec((1, s, w), lambda i, j: (i, 0, 0))
    return pl.pallas_call(
        functools.partial(_dsa_kernel, seq=s, topk=topk),
        grid=(b, nq),
        in_specs=[qspec(EV_AQ), qspec(EV_IQ), qspec(LANES), kspec(IDX_DIM), kspec(HEAD_DIM),
                  pl.BlockSpec((HEAD_DIM, s), lambda i, j: (0, i)),
                  pl.BlockSpec((2 * KC, A_HEADS * TQ), lambda i, j: (0, 0))],
        out_specs=pl.BlockSpec((1, TQ, EV_AQ), lambda i, j: (i, j, 0)),
        out_shape=jax.ShapeDtypeStruct((b, s, EV_AQ), CDT),
        scratch_shapes=[pltpu.VMEM((s, TQ), I32), pltpu.VMEM((8, TQ), I32)],
        compiler_params=_cparams(("parallel", "arbitrary")),
        name="dsa_attention",
    )(aqt, iqt, smallt, ik, ak, avt, tokbt)


def _nsa_kernel(bqt_ref, smallt_ref, ck_ref, cv_ref, sk_ref, svt_ref, wk_ref, wvt_ref, tokbt_ref, cmpbt_ref, o_ref,
                *, seq):
    m = pl.program_id(1)
    scale = HEAD_DIM ** -0.5
    n_c = seq // CMP_STRIDE
    n_sb = seq // SEL_BLOCK
    lanes4 = B_HEADS * TQ
    sel_shift = SEL_BLOCK.bit_length() - 1
    q_t = jnp.concatenate([bqt_ref[h * HEAD_DIM:(h + 1) * HEAD_DIM, :] for h in range(B_HEADS)], axis=1)
    t_pos = m * TQ + lax.broadcasted_iota(I32, (1, TQ), 1)
    kidx = lax.broadcasted_iota(I32, (KC, 1), 0)

    def chunk(c):
        return pl.ds(pl.multiple_of(c * KC, KC), KC)

    lc = _dot(ck_ref[0], q_t) * scale
    n_i = lax.broadcasted_iota(I32, (n_c, LANES), 0)
    r_i = lax.broadcasted_iota(I32, (n_c, LANES), 1)
    first_n = m * (TQ // CMP_STRIDE) - TQ // CMP_STRIDE
    place = jnp.where(n_i == first_n + r_i, 1.0, 0.0).astype(CDT)
    terms = _split_terms(cmpbt_ref[...], 2)
    lc = lc + _dot(jnp.concatenate([place] * len(terms), axis=1), jnp.concatenate(terms, axis=0))
    n_col = lax.broadcasted_iota(I32, (n_c, 1), 0)
    cvalid = (n_col * CMP_STRIDE + (CMP_LEN - 1) <= t_pos) & (n_col < n_c - 1)
    cv4 = _tile_heads(cvalid, B_HEADS)
    sc = jnp.where(cv4, lc, NEG_BIG)
    pc = jnp.where(cv4, jnp.exp(sc - jnp.max(sc, axis=0, keepdims=True)), 0.0)
    den = jnp.sum(pc, axis=0, keepdims=True)
    pc = jnp.where(den > 0.0, pc / jnp.where(den > 0.0, den, 1.0), 0.0)
    cv_t = cv_ref[0].astype(F32).T.astype(CDT)
    oc = _dot(cv_t, pc.astype(CDT))

    psum = pc[:, 0:TQ]
    for h in range(1, B_HEADS):
        psum = psum + pc[:, h * TQ:(h + 1) * TQ]
    om = lax.broadcasted_iota(I32, (LANES, n_c), 0) * SEL_BLOCK
    on = lax.broadcasted_iota(I32, (LANES, n_c), 1) * CMP_STRIDE
    overlap = jnp.where((on < om + SEL_BLOCK) & (on + CMP_LEN > om) & (om < seq), 1.0, 0.0).astype(CDT)
    terms = _split_terms(psum, 3)
    imp = _dot(jnp.concatenate([overlap] * len(terms), axis=1), jnp.concatenate(terms, axis=0))
    blk = lax.broadcasted_iota(I32, (LANES, 1), 0)
    cur = t_pos >> sel_shift
    forced = (blk == 0) | (blk == cur) | (blk == cur - 1)
    val = jnp.where(forced, jnp.inf, jnp.where(blk > cur, -jnp.inf, imp))
    blk_f = blk.astype(F32)
    alive = blk < n_sb
    sel = jnp.zeros((LANES, TQ), F32)
    for _ in range(min(N_SEL, n_sb)):
        best = jnp.max(jnp.where(alive, val, -jnp.inf), axis=0, keepdims=True)
        first = jnp.min(jnp.where(alive & (val == best), blk_f, float(LANES)), axis=0, keepdims=True)
        pick = blk_f == first
        sel = jnp.where(pick, 1.0, sel)
        alive = alive & jnp.logical_not(pick)
    sel_c = sel.astype(CDT)

    def near_bias(near):
        return tokbt_ref[near * KC:(near + 1) * KC, :]

    def sel_attend(c, carry, near):
        logits = _dot(sk_ref[0, chunk(c), :], q_t) * scale
        e_key = lax.broadcasted_iota(I32, (KC, LANES), 0)
        e_blk = lax.broadcasted_iota(I32, (KC, LANES), 1)
        expand = jnp.where(e_blk == c * (KC // SEL_BLOCK) + (e_key >> sel_shift), 1.0, 0.0).astype(CDT)
        mask = _dot(expand, sel_c) > 0.5
        if near is not None:
            logits = logits + near_bias(near)
            mask = mask & (c * KC + kidx <= t_pos)
        return _softmax_step(carry, logits, _tile_heads(mask, B_HEADS), svt_ref[:, chunk(c)])

    cs = _softmax_init(lanes4)
    cs = lax.fori_loop(0, jnp.maximum(m - 1, 0), lambda c, cr: sel_attend(c, cr, None), cs)
    cs = lax.cond(m >= 1, lambda cr: sel_attend(m - 1, cr, 0), lambda cr: cr, cs)
    cs = sel_attend(m, cs, 1)
    osel = _softmax_out(cs)

    def win_attend(c, carry, near):
        logits = _dot(wk_ref[0, chunk(c), :], q_t) * scale
        dist = t_pos - (c * KC + kidx)
        mask = (dist >= 0) & (dist < WINDOW)
        if near is not None:
            logits = logits + near_bias(near)
        return _softmax_step(carry, logits, _tile_heads(mask, B_HEADS), wvt_ref[:, chunk(c)])

    cw = _softmax_init(lanes4)
    for back in range(WINDOW // KC, 1, -1):
        cw = lax.cond(m >= back, lambda cr, back=back: win_attend(m - back, cr, None), lambda cr: cr, cw)
    cw = lax.cond(m >= 1, lambda cr: win_attend(m - 1, cr, 0), lambda cr: cr, cw)
    cw = win_attend(m, cw, 1)
    ow = _softmax_out(cw)

    g0 = SM_G - SM_IW
    gate = jax.nn.sigmoid(smallt_ref[SM_IW:SM_G + 3 * B_HEADS, :])
    outs = []
    for h in range(B_HEADS):
        cols = slice(h * TQ, (h + 1) * TQ)
        outs.append(gate[g0 + h:g0 + h + 1] * oc[:, cols]
                    + gate[g0 + B_HEADS + h:g0 + B_HEADS + h + 1] * osel[:, cols]
                    + gate[g0 + 2 * B_HEADS + h:g0 + 2 * B_HEADS + h + 1] * ow[:, cols])
    o_ref[0] = _heads_to_rows(jnp.concatenate(outs, axis=1), B_HEADS).astype(o_ref.dtype)


def _nsa(bqt, smallt, ck, cv, sk, svt, wk, wvt, tokbt, cmpbt):
    b, s, _ = sk.shape
    nq = s // TQ
    n_c = s // CMP_STRIDE
    qspec = lambda rows: pl.BlockSpec((rows, TQ), lambda i, j: (0, i * nq + j))
    kspec = lambda n: pl.BlockSpec((1, n, HEAD_DIM), lambda i, j: (i, 0, 0))
    vspec = pl.BlockSpec((HEAD_DIM, s), lambda i, j: (0, i))
    return pl.pallas_call(
        functools.partial(_nsa_kernel, seq=s),
        grid=(b, nq),
        in_specs=[qspec(EV_BQ), qspec(LANES), kspec(n_c), kspec(n_c), kspec(s), vspec, kspec(s), vspec,
                  pl.BlockSpec((2 * KC, B_HEADS * TQ), lambda i, j: (0, 1)),
                  pl.BlockSpec((LANES, B_HEADS * TQ), lambda i, j: (0, 1))],
        out_specs=pl.BlockSpec((1, TQ, EV_BQ), lambda i, j: (i, j, 0)),
        out_shape=jax.ShapeDtypeStruct((b, s, EV_BQ), CDT),
        compiler_params=_cparams(("parallel", "arbitrary")),
        name="nsa_attention",
    )(bqt, smallt, ck, cv, sk, svt, wk, wvt, tokbt, cmpbt)


def _proj_resid_kernel(*refs, n_in):
    a_refs, w_refs, r_ref, o_ref = refs[:n_in], refs[n_in:2 * n_in], refs[2 * n_in], refs[2 * n_in + 1]
    acc = r_ref[...]
    for a_ref, w_ref in zip(a_refs, w_refs):
        acc = acc + _dot(a_ref[...], w_ref[...])
    o_ref[...] = acc


def _proj_resid(acts, weights, resid, tm=512):
    m, d = resid.shape
    row = lambda width: pl.BlockSpec((tm, width), lambda i: (i, 0))
    full = lambda a: pl.BlockSpec(a.shape, lambda i: (0, 0))
    weights = [w.astype(CDT) for w in weights]
    return pl.pallas_call(
        functools.partial(_proj_resid_kernel, n_in=len(acts)),
        grid=(m // tm,),
        in_specs=[row(a.shape[1]) for a in acts] + [full(w) for w in weights] + [row(d)],
        out_specs=row(d),
        out_shape=jax.ShapeDtypeStruct((m, d), F32),
        compiler_params=_cparams(("parallel",)),
        name="proj_residual",
    )(*acts, *weights, resid)


def _rms_matmul_kernel(x_ref, g_ref, w_ref, o_ref):
    h = _rms(x_ref[...], g_ref[...]).astype(CDT)
    o_ref[...] = _dot(h, w_ref[...]).astype(o_ref.dtype)


def _rms_matmul(x2d, g, w, tm=512, tn=1024):
    m, d = x2d.shape
    n = w.shape[1]
    return pl.pallas_call(
        _rms_matmul_kernel,
        grid=(m // tm, n // tn),
        in_specs=[pl.BlockSpec((tm, d), lambda i, j: (i, 0)),
                  pl.BlockSpec((1, d), lambda i, j: (0, 0)),
                  pl.BlockSpec((d, tn), lambda i, j: (0, j))],
        out_specs=pl.BlockSpec((tm, tn), lambda i, j: (i, j)),
        out_shape=jax.ShapeDtypeStruct((m, n), CDT),
        compiler_params=_cparams(("parallel", "parallel")),
        name="rms_matmul",
    )(x2d, g.reshape(1, d), w.astype(CDT))


def _stick_kernel(q_ref, k_ref, v_ref, o_ref):
    m = pl.program_id(2)
    scale = HEAD_DIM ** -0.5 * math.log2(math.e)
    q = q_ref[0]
    per = SB_T // SB_G
    t_pos = m * SB_T + lax.broadcasted_iota(I32, (SB_T, 1), 0)
    lane = lax.broadcasted_iota(I32, (1, SB_T), 1)
    uj = lax.broadcasted_iota(I32, (SB_G, SB_G), 0)
    us = lax.broadcasted_iota(I32, (SB_G, SB_G), 1)
    suffix = jnp.where(uj > us, 1.0, 0.0).astype(CDT)

    def head_chunk(hh, c, carry, masked):
        run, acc = carry
        rows = pl.ds(pl.multiple_of(c * SB_T, SB_T), SB_T)
        cols = slice(hh * HEAD_DIM, (hh + 1) * HEAD_DIM)
        z = _dot_nt(q[:, cols], k_ref[0, rows, cols]) * scale
        neg_z = -z
        log_1m = jnp.minimum(neg_z, 0.0) - jnp.log2(1.0 + jnp.exp2(jnp.minimum(z, neg_z)))
        log_b = z + log_1m
        if masked:
            strict = c * SB_T + lane < t_pos
            log_1m = jnp.where(strict, log_1m, 0.0)
        terms = jnp.concatenate([log_1m[:, g * SB_G:(g + 1) * SB_G] for g in range(per)], axis=0).astype(CDT)
        sums = _dot(terms, suffix)
        later = [None] * per
        for g in range(per - 1, -1, -1):
            block = slice(g * SB_T, (g + 1) * SB_T)
            later[g] = run + sums[block]
            run = run + sums[block, 0:1] + terms[block, 0:1].astype(F32)
        a = jnp.exp2(log_b + jnp.concatenate(later, axis=1))
        if masked:
            a = jnp.where(strict, a, 0.0)
        acc = acc + _dot(a.astype(CDT), v_ref[0, rows, cols])
        return run, acc

    def chunk(c, carry, masked):
        return tuple(head_chunk(hh, c, carry[hh], masked) for hh in range(SB_HEADS))

    def any_live(carry):
        top = carry[0][0]
        for cr in carry[1:]:
            top = jnp.maximum(top, cr[0])
        return (jnp.max(top) > SB_DEAD).astype(I32)

    def far_cond(state):
        i, live, _ = state
        return (i < m) & (live > 0)

    def far_body(state):
        i, _, carry = state
        carry = chunk(m - 1 - i, carry, False)
        return i + 1, any_live(carry), carry

    carry = tuple((jnp.zeros((SB_T, 1), F32), jnp.zeros((SB_T, HEAD_DIM), F32)) for _ in range(SB_HEADS))
    carry = chunk(m, carry, True)
    _, _, carry = lax.while_loop(far_cond, far_body, (jnp.int32(0), any_live(carry), carry))
    o_ref[0] = jnp.concatenate([cr[1] for cr in carry], axis=1).astype(o_ref.dtype)


def _stick(qkv, b, s):
    h = C_HEADS // SB_HEADS
    width = SB_HEADS * HEAD_DIM
    return pl.pallas_call(
        _stick_kernel,
        grid=(b, h, s // SB_T),
        in_specs=[pl.BlockSpec((1, SB_T, width), lambda i, j, k: (i, k, j)),
                  pl.BlockSpec((1, s, width), lambda i, j, k: (i, 0, h + j)),
                  pl.BlockSpec((1, s, width), lambda i, j, k: (i, 0, 2 * h + j))],
        out_specs=pl.BlockSpec((1, SB_T, width), lambda i, j, k: (i, k, j)),
        out_shape=jax.ShapeDtypeStruct((b, s, C_HEADS * HEAD_DIM), CDT),
        compiler_params=_cparams(("parallel", "parallel", "arbitrary")),
        name="stick_breaking",
    )(qkv, qkv, qkv)


def _ffn_kernel(*refs, final):
    it = iter(refs)
    x_ref, g_ref, wg_ref, wu_ref, wd_ref = next(it), next(it), next(it), next(it), next(it)
    fg_ref = next(it) if final else None
    o_ref, h_scr, acc_scr = next(it), next(it), next(it)
    f = pl.program_id(1)

    @pl.when(f == 0)
    def _():
        h_scr[...] = _rms(x_ref[...], g_ref[...]).astype(CDT)
        acc_scr[...] = jnp.zeros(acc_scr.shape, F32)

    h = h_scr[...]
    act = jax.nn.silu(_dot(h, wg_ref[...])) * _dot(h, wu_ref[...])
    acc_scr[...] += _dot(act.astype(CDT), wd_ref[...])

    @pl.when(f == pl.num_programs(1) - 1)
    def _():
        out = x_ref[...] + acc_scr[...]
        if final:
            out = _rms(out, fg_ref[...])
        o_ref[...] = out


def _ffn(x2d, g, w_gate, w_up, w_down, final_g=None, tm=512, tf=None):
    m, d = x2d.shape
    ff = w_gate.shape[1]
    final = final_g is not None
    row = pl.BlockSpec((tm, d), lambda i, f: (i, 0))
    vec = pl.BlockSpec((1, d), lambda i, f: (0, 0))
    ins = [x2d, g.reshape(1, d), w_gate.astype(CDT), w_up.astype(CDT), w_down.astype(CDT)]
    specs = [row, vec, pl.BlockSpec((d, tf), lambda i, f: (0, f)), pl.BlockSpec((d, tf), lambda i, f: (0, f)),
             pl.BlockSpec((tf, d), lambda i, f: (f, 0))]
    if final:
        ins.append(final_g.reshape(1, d))
        specs.append(vec)
    return pl.pallas_call(
        functools.partial(_ffn_kernel, final=final),
        grid=(m // tm, ff // tf),
        in_specs=specs,
        out_specs=row,
        out_shape=jax.ShapeDtypeStruct((m, d), F32),
        scratch_shapes=[pltpu.VMEM((tm, d), CDT), pltpu.VMEM((tm, d), F32)],
        compiler_params=_cparams(("parallel", "arbitrary")),
        name="swiglu",
    )(*ins)


MOE_TM = 512
MOE_TT = 512
RT_E1, RT_E2, RT_W1, RT_W2, RT_R1, RT_R2 = range(6)


def _router_kernel(x_ref, g_ref, wr_ref, h_ref, meta_ref, cnt_ref, base_scr):
    i = pl.program_id(0)

    @pl.when(i == 0)
    def _():
        base_scr[...] = jnp.zeros(base_scr.shape, F32)

    hn = _rms(x_ref[...], g_ref[...])
    h_ref[...] = hn
    logits = jnp.dot(hn, wr_ref[...], preferred_element_type=F32, precision=lax.Precision.HIGHEST)
    lane = lax.broadcasted_iota(I32, logits.shape, 1)
    logits = jnp.where(lane < N_EXPERTS, logits, -jnp.inf)
    v1 = jnp.max(logits, axis=1, keepdims=True)
    i1 = jnp.min(jnp.where(logits == v1, lane, LANES), axis=1, keepdims=True)
    rest = jnp.where(lane == i1, -jnp.inf, logits)
    v2 = jnp.max(rest, axis=1, keepdims=True)
    i2 = jnp.min(jnp.where(rest == v2, lane, LANES), axis=1, keepdims=True)
    e2 = jnp.exp(v2 - v1)
    den = 1.0 + e2
    oh1, oh2 = lane == i1, lane == i2
    hits = jnp.where(oh1 | oh2, 1.0, 0.0)
    tm = hits.shape[0]
    earlier = jnp.where(lax.broadcasted_iota(I32, (tm, tm), 1) < lax.broadcasted_iota(I32, (tm, tm), 0), 1.0, 0.0)
    before = _dot(earlier.astype(CDT), hits.astype(CDT)) + base_scr[...]
    r1 = jnp.sum(jnp.where(oh1, before, 0.0), axis=1, keepdims=True)
    r2 = jnp.sum(jnp.where(oh2, before, 0.0), axis=1, keepdims=True)
    base_scr[...] += jnp.sum(hits, axis=0, keepdims=True)
    cnt_ref[...] = base_scr[...]
    meta = jnp.zeros(logits.shape, F32)
    for k, v in ((RT_E1, i1.astype(F32)), (RT_E2, i2.astype(F32)), (RT_W1, 1.0 / den), (RT_W2, e2 / den),
                 (RT_R1, r1), (RT_R2, r2)):
        meta = jnp.where(lane == k, v, meta)
    meta_ref[...] = meta


def _router(x2d, g, w_router, tm=512):
    m, d = x2d.shape
    wr = jnp.concatenate([w_router, jnp.zeros((d, LANES - w_router.shape[1]), w_router.dtype)], axis=1)
    return pl.pallas_call(
        _router_kernel,
        grid=(m // tm,),
        in_specs=[pl.BlockSpec((tm, d), lambda i: (i, 0)), pl.BlockSpec((1, d), lambda i: (0, 0)),
                  pl.BlockSpec((d, LANES), lambda i: (0, 0))],
        out_specs=(pl.BlockSpec((tm, d), lambda i: (i, 0)), pl.BlockSpec((tm, LANES), lambda i: (i, 0)),
                   pl.BlockSpec((1, LANES), lambda i: (0, 0))),
        out_shape=(jax.ShapeDtypeStruct((m, d), F32), jax.ShapeDtypeStruct((m, LANES), F32),
                   jax.ShapeDtypeStruct((1, LANES), F32)),
        scratch_shapes=[pltpu.VMEM((1, LANES), F32)],
        compiler_params=_cparams(("arbitrary",)),
        name="moe_router",
    )(x2d, g.reshape(1, d), wr)


MOE_UNROLL = 8


def _row_copy(src, src_row, dst, dst_row, sem):
    return pltpu.make_async_copy(src.at[pl.ds(src_row, 1), :], dst.at[pl.ds(dst_row, 1), :], sem)


def _dispatch_kernel(p1_ref, p2_ref, h_ref, xs_in_hbm, xs_hbm, sem):
    del xs_in_hbm

    def start(r, carry):
        _row_copy(h_ref, r, xs_hbm, p1_ref[0, 0, r], sem).start(priority=0)
        _row_copy(h_ref, r, xs_hbm, p2_ref[0, 0, r], sem).start(priority=1)
        return carry

    def wait(r, carry):
        _row_copy(h_ref, 0, xs_hbm, 0, sem).wait()
        _row_copy(h_ref, 0, xs_hbm, 0, sem).wait()
        return carry

    lax.fori_loop(0, MOE_TT, start, 0, unroll=MOE_UNROLL)
    lax.fori_loop(0, MOE_TT, wait, 0, unroll=MOE_UNROLL)


def _dispatch(h, p1, p2, n_rows):
    m, d = h.shape
    tok = pl.BlockSpec((1, 1, MOE_TT), lambda i: (i, 0, 0), memory_space=pltpu.SMEM)
    anyspec = pl.BlockSpec(memory_space=pl.ANY)
    return pl.pallas_call(
        _dispatch_kernel,
        grid=(m // MOE_TT,),
        in_specs=[tok, tok, pl.BlockSpec((MOE_TT, d), lambda i: (i, 0)), anyspec],
        out_specs=anyspec,
        scratch_shapes=[pltpu.SemaphoreType.DMA(())],
        out_shape=jax.ShapeDtypeStruct((n_rows, d), F32),
        input_output_aliases={3: 0},
        compiler_params=_cparams(("arbitrary",)),
        name="moe_dispatch",
    )(p1, p2, h, jnp.zeros((n_rows, d), F32))


def _expert_ffn_kernel(te_ref, nu_ref, xs_ref, wg_ref, wu_ref, wd_ref, y_ref, h_scr, acc_scr):
    del te_ref
    i, f = pl.program_id(0), pl.program_id(1)
    used = i < nu_ref[0]

    @pl.when(used & (f == 0))
    def _():
        h_scr[...] = xs_ref[...].astype(CDT)
        acc_scr[...] = jnp.zeros(acc_scr.shape, F32)

    @pl.when(used)
    def _():
        h = h_scr[...]
        act = jax.nn.silu(_dot(h, wg_ref[0])) * _dot(h, wu_ref[0])
        acc_scr[...] += _dot(act.astype(CDT), wd_ref[0])

    @pl.when(f == pl.num_programs(1) - 1)
    def _():
        y_ref[...] = jnp.where(used, acc_scr[...], 0.0)


def _expert_ffn(xs, tile_expert, n_used, w_gate, w_up, w_down, tf):
    n_rows, d = xs.shape
    ff = w_gate.shape[2]
    row = pl.BlockSpec((MOE_TM, d), lambda i, f, te, nu: (i, 0))
    return pl.pallas_call(
        _expert_ffn_kernel,
        grid_spec=pltpu.PrefetchScalarGridSpec(
            num_scalar_prefetch=2, grid=(n_rows // MOE_TM, ff // tf),
            in_specs=[row,
                      pl.BlockSpec((1, d, tf), lambda i, f, te, nu: (te[i], 0, f)),
                      pl.BlockSpec((1, d, tf), lambda i, f, te, nu: (te[i], 0, f)),
                      pl.BlockSpec((1, tf, d), lambda i, f, te, nu: (te[i], f, 0))],
            out_specs=row,
            scratch_shapes=[pltpu.VMEM((MOE_TM, d), CDT), pltpu.VMEM((MOE_TM, d), F32)]),
        out_shape=jax.ShapeDtypeStruct((n_rows, d), F32),
        compiler_params=_cparams(("arbitrary", "arbitrary")),
        name="moe_expert_swiglu",
    )(tile_expert, n_used, xs, w_gate.astype(CDT), w_up.astype(CDT), w_down.astype(CDT))


def _combine_kernel(p1_ref, p2_ref, x_ref, meta_ref, fg_ref, y_hbm, o_ref, buf1, buf2, sem, *, final):
    def start(r, carry):
        _row_copy(y_hbm, p1_ref[0, 0, r], buf1, r, sem).start(priority=0)
        _row_copy(y_hbm, p2_ref[0, 0, r], buf2, r, sem).start(priority=1)
        return carry

    def wait(r, carry):
        _row_copy(y_hbm, 0, buf1, 0, sem).wait()
        _row_copy(y_hbm, 0, buf2, 0, sem).wait()
        return carry

    lax.fori_loop(0, MOE_TT, start, 0, unroll=MOE_UNROLL)
    lax.fori_loop(0, MOE_TT, wait, 0, unroll=MOE_UNROLL)
    meta = meta_ref[...]
    out = x_ref[...] + (meta[:, RT_W1:RT_W1 + 1] * buf1[...] + meta[:, RT_W2:RT_W2 + 1] * buf2[...])
    if final:
        out = _rms(out, fg_ref[...])
    o_ref[...] = out


def _combine(x2d, meta, y, p1, p2, final_g):
    m, d = x2d.shape
    final = final_g is not None
    fg = (final_g if final else jnp.ones((d,), F32)).reshape(1, d)
    tok = pl.BlockSpec((1, 1, MOE_TT), lambda i: (i, 0, 0), memory_space=pltpu.SMEM)
    return pl.pallas_call(
        functools.partial(_combine_kernel, final=final),
        grid=(m // MOE_TT,),
        in_specs=[tok, tok,
                  pl.BlockSpec((MOE_TT, d), lambda i: (i, 0)),
                  pl.BlockSpec((MOE_TT, LANES), lambda i: (i, 0)),
                  pl.BlockSpec((1, d), lambda i: (0, 0)),
                  pl.BlockSpec(memory_space=pl.ANY)],
        out_specs=pl.BlockSpec((MOE_TT, d), lambda i: (i, 0)),
        scratch_shapes=[pltpu.VMEM((MOE_TT, d), F32), pltpu.VMEM((MOE_TT, d), F32), pltpu.SemaphoreType.DMA(())],
        out_shape=jax.ShapeDtypeStruct((m, d), F32),
        compiler_params=_cparams(("arbitrary",)),
        name="moe_combine",
    )(p1, p2, x2d, meta, fg, y)


def _moe(x2d, g, w_router, w_gate, w_up, w_down, final_g, tf):
    m, d = x2d.shape
    n_e = w_gate.shape[0]
    h, meta, cnt = _router(x2d, g, w_router)
    n_tiles = (2 * m) // MOE_TM + n_e
    experts = jnp.arange(n_e, dtype=I32)
    padded = ((cnt[0, :n_e].astype(I32) + MOE_TM - 1) // MOE_TM) * MOE_TM
    ends = jnp.sum(jnp.where(experts[None, :] <= experts[:, None], padded[None, :], 0), axis=1)
    off = ends - padded
    tile_start = jnp.arange(n_tiles, dtype=I32) * MOE_TM
    tile_expert = jnp.minimum(jnp.sum((tile_start[:, None] >= ends[None, :]).astype(I32), axis=1), n_e - 1)
    n_used = (ends[n_e - 1:] // MOE_TM).astype(I32)

    def position(e_lane, r_lane):
        e = meta[:, e_lane].astype(I32)
        seg = jnp.sum(jnp.where(e[:, None] == experts[None, :], off[None, :], 0), axis=1)
        return (seg + meta[:, r_lane].astype(I32)).reshape(m // MOE_TT, 1, MOE_TT)

    p1, p2 = position(RT_E1, RT_R1), position(RT_E2, RT_R2)
    xs = _dispatch(h, p1, p2, n_tiles * MOE_TM)
    y = _expert_ffn(xs, tile_expert, n_used, w_gate, w_up, w_down, tf)
    return _combine(x2d, meta, y, p1, p2, final_g)


def _even_mixer(x2d, b, s, g, w_in, kv_norm, w_uk, w_uv, cmp_pos, cmp_w1, cmp_w2, w_out, tokbt, cmpbt):
    (aqt, iqt, smallt, bqt, svt, wvt, avt, ak, ik, ckr, cvr, sk, wk) = _even_inproj(
        x2d, g, w_in, kv_norm, w_uk, w_uv)
    r3 = lambda a: a.reshape(b, s, a.shape[1])
    oa = _dsa(aqt, iqt, smallt, r3(ik), r3(ak), avt, tokbt)
    cmp = _compress(jnp.stack([ckr, cvr]).reshape(2, b, s, HEAD_DIM), cmp_pos, cmp_w1, cmp_w2)
    ob = _nsa(bqt, smallt, cmp[0], cmp[1], r3(sk), svt, r3(wk), wvt, tokbt, cmpbt)
    return _proj_resid([oa.reshape(b * s, EV_AQ), ob.reshape(b * s, EV_BQ)],
                       [w_out[:EV_AQ], w_out[EV_AQ:]], x2d)


def kernel(x, rel_bias, ev_norm_mix, ev_w_in, ev_kv_norm, ev_w_uk, ev_w_uv, ev_cmp_pos, ev_cmp_w1, ev_cmp_w2, ev_w_out, ev_norm_ffn, ev_w_gate, ev_w_up, ev_w_down, od_norm_mix, od_w_qkv, od_w_out, od_norm_ffn, od_w_router, od_w_gate, od_w_up, od_w_down, final_norm):
    b, s, d = x.shape
    depth = ev_norm_mix.shape[0] + od_norm_mix.shape[0]
    tokbt, cmpbt = _bias_tiles(rel_bias)
    x2d = x.reshape(b * s, d)
    for layer in range(depth):
        i = layer // 2
        last = layer == depth - 1
        if layer % 2 == 0:
            x2d = _even_mixer(x2d, b, s, ev_norm_mix[i], ev_w_in[i], ev_kv_norm[i], ev_w_uk[i], ev_w_uv[i],
                              ev_cmp_pos[i], ev_cmp_w1[i], ev_cmp_w2[i], ev_w_out[i], tokbt, cmpbt)
            x2d = _ffn(x2d, ev_norm_ffn[i], ev_w_gate[i], ev_w_up[i], ev_w_down[i],
                       final_g=final_norm if last else None, tf=ev_w_gate.shape[2] // 2)
        else:
            qkv = _rms_matmul(x2d, od_norm_mix[i], od_w_qkv[i])
            o = _stick(qkv.reshape(b, s, qkv.shape[1]), b, s)
            x2d = _proj_resid([o.reshape(b * s, o.shape[2])], [od_w_out[i]], x2d)
            x2d = _moe(x2d, od_norm_ffn[i], od_w_router[i], od_w_gate[i], od_w_up[i], od_w_down[i],
                       final_norm if last else None, tf=od_w_gate.shape[3] // 2)
    return x2d.reshape(b, s, d)
```
